```python
import math
import jax, jax.numpy as jnp
from jax import lax
import numpy as np

D_MODEL = 1024
BATCH = 2
SEQ = 8192
DEPTH = 1

N_MEM = 256
EPS = 1e-5
Q_BLOCK = 128
NEG = -1e30
DIFF_HEADS = 8
DIFF_HD = 64
DIFF_W = DIFF_HEADS * 2 * DIFF_HD
DIL_GROUPS = ((128, 1), (512, 4), (2048, 16))
DIL_HEADS = 4
DIL_HD = 128
DIL_W = len(DIL_GROUPS) * DIL_HEADS * DIL_HD
DIL_OUT = DIL_HEADS * DIL_HD
X_HEADS = 4
X_HD = D_MODEL // X_HEADS
D_FF = 2816
CONV_W = 3
IN_SIZES = (DIFF_W, DIFF_W, DIFF_W, DIL_W, DIL_W, DIL_W, 2 * D_MODEL)
N_IN = sum(IN_SIZES)
SPLIT_POINTS = tuple(sum(IN_SIZES[:i + 1]) for i in range(len(IN_SIZES) - 1))

kernel_name = "hybrid_diff_dilated_gated_block"


def rmsnorm(x, g):
    x32 = x.astype(jnp.float32)
    y = x32 * lax.rsqrt(jnp.mean(x32 * x32, axis=-1, keepdims=True) + EPS)
    return (y * g.astype(jnp.float32)).astype(x.dtype)


def alibi_slopes(n):
    return jnp.exp2(-8.0 * jnp.arange(1, n + 1, dtype=jnp.float32) / n)


def diff_attention(q, k, v, lam, sub_g, lam_init):
    B, S, H, _, E = q.shape
    nb = S // Q_BLOCK
    scale = DIFF_HD ** -0.5
    kh = k.transpose(0, 2, 3, 1, 4)
    vh = v.transpose(0, 2, 1, 3)
    qblk = q.transpose(0, 2, 3, 1, 4).reshape(B, H, 2, nb, Q_BLOCK, E)
    qblk = jnp.moveaxis(qblk, 3, 0)
    slopes = alibi_slopes(H)
    kpos = jnp.arange(S)

    def one_block(args):
        qb, i = args
        qpos = i * Q_BLOCK + jnp.arange(Q_BLOCK)
        dist = qpos[:, None] - kpos[None, :]
        s = jnp.einsum('bhcqe,bhcke->bhcqk', qb, kh).astype(jnp.float32) * scale
        s = s - slopes[:, None, None, None] * dist.astype(jnp.float32)
        s = jnp.where(dist >= 0, s, NEG)
        p = jax.nn.softmax(s, axis=-1)
        a = p[:, :, 0] - lam * p[:, :, 1]
        return jnp.einsum('bhqk,bhke->bhqe', a.astype(vh.dtype), vh)

    o = lax.map(one_block, (qblk, jnp.arange(nb)))
    o = jnp.moveaxis(o, 0, 2).reshape(B, H, S, 2 * E).transpose(0, 2, 1, 3)
    o = rmsnorm(o, sub_g) * (1.0 - lam_init)
    return o.reshape(B, S, H * 2 * E)


def dilated_group(q, k, v, window, dil):
    B, S, H, E = q.shape
    blk = window // dil
    span = blk * dil
    Sp = -(-S // span) * span
    nb = Sp // span
    scale = E ** -0.5
    pad = ((0, 0), (0, Sp - S), (0, 0), (0, 0))

    def to_blocks(t):
        return jnp.pad(t, pad).reshape(B, nb, blk, dil, H, E)

    def with_prev(t):
        prev = jnp.pad(t, ((0, 0), (1, 0), (0, 0), (0, 0), (0, 0), (0, 0)))[:, :-1]
        return jnp.concatenate([prev, t], axis=2)

    qb = to_blocks(q)
    kk = with_prev(to_blocks(k))
    vv = with_prev(to_blocks(v))
    s = jnp.einsum('bnqrhe,bnkrhe->bnrhqk', qb, kk).astype(jnp.float32) * scale
    qi = jnp.arange(blk)[:, None]
    ki = jnp.arange(2 * blk)[None, :]
    step = blk + qi - ki
    n_idx = jnp.arange(nb)[:, None, None]
    valid = (step >= 0) & (step <= blk) & (n_idx * blk + ki - blk >= 0)
    slopes = alibi_slopes(H) / dil
    s = s - slopes[:, None, None] * (dil * step).astype(jnp.float32)
    s = jnp.where(valid[:, None, None], s, NEG)
    m = jnp.max(s, axis=-1, keepdims=True)
    p = jnp.exp(s - m)
    den = jnp.sum(p, axis=-1, keepdims=True)
    o = jnp.einsum('bnrhqk,bnkrhe->bnqrhe', (p / den).astype(v.dtype), vv)
    lse = (m + jnp.log(den))[..., 0]
    o = o.reshape(B, Sp, H, E)[:, :S]
    lse = lse.transpose(0, 1, 4, 2, 3).reshape(B, Sp, H)[:, :S]
    return o, lse


def dilated_mixer(q, k, v):
    B, S, G, H, E = q.shape
    outs, lses = [], []
    for g, (window, dil) in enumerate(DIL_GROUPS):
        o, lse = dilated_group(q[:, :, g], k[:, :, g], v[:, :, g], window, dil)
        outs.append(o)
        lses.append(lse)
    alpha = jax.nn.softmax(jnp.stack(lses, axis=0), axis=0)
    o = jnp.sum(alpha[..., None] * jnp.stack(outs, axis=0).astype(jnp.float32), axis=0)
    return o.astype(q.dtype).reshape(B, S, H * E)


def causal_dwconv(a, w, b):
    C = a.shape[-1]
    y = lax.conv_general_dilated(a, w[:, None, :], window_strides=(1,),
                                 padding=[(CONV_W - 1, 0)],
                                 dimension_numbers=('NWC', 'WIO', 'NWC'),
                                 feature_group_count=C)
    return y + b


def setup_inputs(seed: int = 0) -> dict:
    key = jax.random.key(seed)
    ks = jax.random.split(key, 24)
    f32 = jnp.float32

    def w(k, fan_in, fan_out):
        return jax.random.normal(k, (DEPTH, fan_in, fan_out), f32) * fan_in ** -0.5

    def gain(k, shape):
        return 1.0 + 0.02 * jax.random.normal(k, shape, f32)

    return {
        "x": jax.random.normal(ks[0], (BATCH, SEQ, D_MODEL), f32),
        "mem": jax.random.normal(ks[1], (BATCH, N_MEM, D_MODEL), f32),
        "mix_norm_g": gain(ks[2], (DEPTH, D_MODEL)),
        "w_in": w(ks[3], D_MODEL, N_IN),
        "b_gate": 0.02 * jax.random.normal(ks[4], (DEPTH, 2 * D_MODEL), f32),
        "lam_q1": 0.1 * jax.random.normal(ks[5], (DEPTH, DIFF_HD), f32),
        "lam_k1": 0.1 * jax.random.normal(ks[6], (DEPTH, DIFF_HD), f32),
        "lam_q2": 0.1 * jax.random.normal(ks[7], (DEPTH, DIFF_HD), f32),
        "lam_k2": 0.1 * jax.random.normal(ks[8], (DEPTH, DIFF_HD), f32),
        "diff_subln_g": gain(ks[9], (DEPTH, 2 * DIFF_HD)),
        "w_branch_diff": w(ks[10], DIFF_W, D_MODEL),
        "w_branch_dil": w(ks[11], DIL_OUT, D_MODEL),
        "w_out": w(ks[12], D_MODEL, D_MODEL),
        "cross_norm_g": gain(ks[13], (DEPTH, D_MODEL)),
        "mem_norm_g": gain(ks[14], (DEPTH, D_MODEL)),
        "w_cq": w(ks[15], D_MODEL, D_MODEL),
        "w_ckv": w(ks[16], D_MODEL, 2 * D_MODEL),
        "w_co": w(ks[17], D_MODEL, D_MODEL),
        "ffn_norm_g": gain(ks[18], (DEPTH, D_MODEL)),
        "w_up": w(ks[19], D_MODEL, 2 * D_FF),
        "conv_w": jax.random.normal(ks[20], (DEPTH, CONV_W, D_FF), f32) * CONV_W ** -0.5,
        "conv_b": 0.02 * jax.random.normal(ks[21], (DEPTH, D_FF), f32),
        "w_down": w(ks[22], D_FF, D_MODEL),
        "final_norm_g": gain(ks[23], (D_MODEL,)),
    }


def reference(x, mem, mix_norm_g, w_in, b_gate, lam_q1, lam_k1, lam_q2, lam_k2,
              diff_subln_g, w_branch_diff, w_branch_dil, w_out, cross_norm_g,
              mem_norm_g, w_cq, w_ckv, w_co, ffn_norm_g, w_up, conv_w, conv_b,
              w_down, final_norm_g):
    B, S, D = x.shape
    G = len(DIL_GROUPS)
    for l in range(DEPTH):
        lam_init = 0.8 - 0.6 * math.exp(-0.3 * l)
        h = rmsnorm(x, mix_norm_g[l])
        proj = h @ w_in[l]
        dq, dk, dv, sq, sk, sv, gcols = jnp.split(proj, SPLIT_POINTS, axis=-1)
        lam = (jnp.exp(jnp.sum(lam_q1[l].astype(jnp.float32) * lam_k1[l].astype(jnp.float32)))
               - jnp.exp(jnp.sum(lam_q2[l].astype(jnp.float32) * lam_k2[l].astype(jnp.float32)))
               + lam_init)
        o_diff = diff_attention(dq.reshape(B, S, DIFF_HEADS, 2, DIFF_HD),
                                dk.reshape(B, S, DIFF_HEADS, 2, DIFF_HD),
                                dv.reshape(B, S, DIFF_HEADS, 2 * DIFF_HD),
                                lam, diff_subln_g[l], lam_init)
        o_dil = dilated_mixer(sq.reshape(B, S, G, DIL_HEADS, DIL_HD),
                              sk.reshape(B, S, G, DIL_HEADS, DIL_HD),
                              sv.reshape(B, S, G, DIL_HEADS, DIL_HD))
        g_diff, g_dil = jnp.split(jax.nn.sigmoid(gcols + b_gate[l]), 2, axis=-1)
        y = g_diff * (o_diff @ w_branch_diff[l]) + g_dil * (o_dil @ w_branch_dil[l])
        x = x + y @ w_out[l]
        hq = rmsnorm(x, cross_norm_g[l])
        mn = rmsnorm(mem, mem_norm_g[l])
        q = (hq @ w_cq[l]).reshape(B, S, X_HEADS, X_HD)
        mk, mv = jnp.split(mn @ w_ckv[l], 2, axis=-1)
        mk = mk.reshape(B, -1, X_HEADS, X_HD)
        mv = mv.reshape(B, -1, X_HEADS, X_HD)
        s = jnp.einsum('bshe,bmhe->bhsm', q, mk).astype(jnp.float32) * X_HD ** -0.5
        p = jax.nn.softmax(s, axis=-1)
        o = jnp.einsum('bhsm,bmhe->bshe', p.astype(mv.dtype), mv).reshape(B, S, D)
        x = x + o @ w_co[l]
        h = rmsnorm(x, ffn_norm_g[l])
        a, b = jnp.split(h @ w_up[l], 2, axis=-1)
        a = causal_dwconv(a, conv_w[l], conv_b[l])
        x = x + (jax.nn.gelu(a, approximate=False) * b) @ w_down[l]
    return rmsnorm(x, final_norm_g)
```

```python
import functools
import math

import jax
import jax.numpy as jnp
from jax import lax
from jax.experimental import pallas as pl
from jax.experimental.pallas import tpu as pltpu

F32 = jnp.float32
BF16 = jnp.bfloat16

EPS = 1e-5
NEG = -1e30
LANES = 128

DIFF_HEADS = 8
DIFF_HD = 64
DIFF_W = DIFF_HEADS * 2 * DIFF_HD
DIFF_BLK = 256
DIL_GROUPS = ((128, 1), (512, 4), (2048, 16))
DIL_HEADS = 4
DIL_HD = 128
DIL_W = len(DIL_GROUPS) * DIL_HEADS * DIL_HD
DIL_OUT = DIL_HEADS * DIL_HD
DIL_BLK = 128
DIL_CHUNK = 2048
X_HEADS = 4
CONV_W = 3
CONV_HALO = 8

QKV_W = 3 * DIFF_W + 3 * DIL_W
COLBLK_DK = DIFF_W // LANES
COLBLK_DV = 2 * DIFF_W // LANES
COLBLK_SQ = 3 * DIFF_W // LANES
COLBLK_SK = COLBLK_SQ + DIL_W // LANES
COLBLK_SV = COLBLK_SK + DIL_W // LANES


def _alibi_slopes(n):
    return jnp.exp2(-8.0 * jnp.arange(1, n + 1, dtype=F32) / n)


def _cparams(sem, vmem_mb):
    return pltpu.CompilerParams(dimension_semantics=sem, vmem_limit_bytes=vmem_mb * 1024 * 1024)


def _rms(x, g):
    return x * lax.rsqrt(jnp.mean(x * x, axis=-1, keepdims=True) + EPS) * g


def _inproj_kernel(x_ref, g_ref, w_ref, *rest, gate):
    if gate:
        b_ref, o_ref, h_ref = rest
    else:
        o_ref, h_ref = rest

    @pl.when(pl.program_id(1) == 0)
    def _():
        h_ref[...] = _rms(x_ref[...], g_ref[...]).astype(BF16)

    acc = jnp.dot(h_ref[...], w_ref[...], preferred_element_type=F32)
    if gate:
        acc = jax.nn.sigmoid(acc + b_ref[...])
    o_ref[...] = acc.astype(o_ref.dtype)


def _inproj(x2d, g, w, bias, *, out_dtype, rb, nb, name):
    t, d = x2d.shape
    n = w.shape[1]
    gate = bias is not None
    in_specs = [pl.BlockSpec((rb, d), lambda i, j: (i, 0)),
                pl.BlockSpec((1, d), lambda i, j: (0, 0)),
                pl.BlockSpec((d, nb), lambda i, j: (0, j))]
    args = [x2d, g.reshape(1, d), w]
    if gate:
        in_specs.append(pl.BlockSpec((1, nb), lambda i, j: (0, j)))
        args.append(bias.reshape(1, n))
    return pl.pallas_call(
        functools.partial(_inproj_kernel, gate=gate),
        grid=(t // rb, n // nb),
        in_specs=in_specs,
        out_specs=pl.BlockSpec((rb, nb), lambda i, j: (i, j)),
        out_shape=jax.ShapeDtypeStruct((t, n), out_dtype),
        scratch_shapes=[pltpu.VMEM((rb, d), BF16)],
        compiler_params=_cparams(("parallel", "arbitrary"), 48),
        name=name,
    )(*args)


def _diff_kernel(slopes_ref, q_ref, k_ref, v_ref, lq1_ref, lk1_ref, lq2_ref, lk2_ref, subg_ref,
                 o_ref, kaug_ref, vt_ref, acc_ref, *, seq, lam_init):
    blk = DIFF_BLK
    h = pl.program_id(1)
    i = pl.program_id(2)
    slope = slopes_ref[h]
    lane = lax.broadcasted_iota(jnp.int32, (blk, LANES), 1)

    @pl.when(i == 0)
    def _build():
        jrel = lax.broadcasted_iota(jnp.int32, (blk, LANES), 0).astype(F32)
        val = slope * jrel
        hi = val.astype(BF16).astype(F32)
        lo = val - hi
        zero = jnp.zeros((blk, LANES), F32)

        def body(jb, carry):
            r0 = pl.multiple_of(jb * blk, blk)
            kk = k_ref[pl.ds(r0, blk), :].astype(F32)
            k1 = jnp.where(lane < DIFF_HD, kk,
                           jnp.where(lane == DIFF_HD, hi, jnp.where(lane == DIFF_HD + 1, lo, zero)))
            k2 = jnp.where(lane >= DIFF_HD, kk,
                           jnp.where(lane == 0, hi, jnp.where(lane == 1, lo, zero)))
            kaug_ref[0, jb] = k1.astype(BF16)
            kaug_ref[1, jb] = k2.astype(BF16)
            vt_ref[jb] = v_ref[pl.ds(r0, blk), :].astype(F32).T.astype(BF16)
            return carry

        lax.fori_loop(0, seq // blk, body, 0)

    qq = q_ref[...].astype(F32) * (DIFF_HD ** -0.5)
    one = jnp.ones((blk, LANES), F32)
    zero = jnp.zeros((blk, LANES), F32)
    q1 = jnp.where(lane < DIFF_HD, qq, jnp.where(lane < DIFF_HD + 2, one, zero))
    q2 = jnp.where(lane >= DIFF_HD, qq, jnp.where(lane < 2, one, zero))
    q_t = (q1.T.astype(BF16), q2.T.astype(BF16))

    qpos = lax.broadcasted_iota(jnp.int32, (1, blk), 1).astype(F32)
    key_i = lax.broadcasted_iota(jnp.int32, (blk, blk), 0)
    qry_i = lax.broadcasted_iota(jnp.int32, (blk, blk), 1)

    def block(jb, dist0, state, masked):
        u = slope * (dist0 + qpos)
        new_state = []
        for c in range(2):
            m, l = state[c]
            t = jnp.dot(kaug_ref[c, jb], q_t[c], preferred_element_type=F32)
            if masked:
                t = jnp.where(key_i <= qry_i, t, NEG)
            m_new = jnp.maximum(m, jnp.max(t, axis=0, keepdims=True) - u)
            p = jnp.exp(t - (m_new + u))
            alpha = jnp.exp(m - m_new)
            l_new = alpha * l + jnp.sum(p, axis=0, keepdims=True)
            pv = jnp.dot(vt_ref[jb], p.astype(BF16), preferred_element_type=F32)
            acc_ref[c] = alpha * acc_ref[c] + pv
            new_state.append((m_new, l_new))
        return tuple(new_state)

    acc_ref[...] = jnp.zeros_like(acc_ref)
    init = tuple((jnp.full((1, blk), NEG, F32), jnp.zeros((1, blk), F32)) for _ in range(2))
    state = block(i, jnp.float32(0.0), init, True)

    def body(n, st):
        return block(i - 1 - n, ((n + 1) * blk).astype(F32), st, False)

    (m1, l1), (m2, l2) = lax.fori_loop(0, i, body, state)

    lam = (jnp.exp(jnp.sum(lq1_ref[...] * lk1_ref[...], axis=-1, keepdims=True))
           - jnp.exp(jnp.sum(lq2_ref[...] * lk2_ref[...], axis=-1, keepdims=True)) + lam_init)
    o = acc_ref[0] * (1.0 / l1) - lam * (acc_ref[1] * (1.0 / l2))
    y = o * lax.rsqrt(jnp.mean(o * o, axis=0, keepdims=True) + EPS)
    o_ref[...] = (y.T * subg_ref[...] * (1.0 - lam_init)).astype(o_ref.dtype)


def _diff_attn(qkv, lq1, lk1, lq2, lk2, subg, *, lam_init):
    b, s, _ = qkv.shape
    blk = DIFF_BLK
    nblk = s // blk
    small = lambda w: pl.BlockSpec((1, w), lambda bi, h, i: (0, 0))
    return pl.pallas_call(
        functools.partial(_diff_kernel, seq=s, lam_init=lam_init),
        grid=(b, DIFF_HEADS, nblk),
        in_specs=[pl.BlockSpec(memory_space=pltpu.SMEM),
                  pl.BlockSpec((None, blk, LANES), lambda bi, h, i: (bi, i, h)),
                  pl.BlockSpec((None, s, LANES), lambda bi, h, i: (bi, 0, COLBLK_DK + h)),
                  pl.BlockSpec((None, s, LANES), lambda bi, h, i: (bi, 0, COLBLK_DV + h)),
                  small(DIFF_HD), small(DIFF_HD), small(DIFF_HD), small(DIFF_HD), small(2 * DIFF_HD)],
        out_specs=pl.BlockSpec((None, blk, LANES), lambda bi, h, i: (bi, i, h)),
        out_shape=jax.ShapeDtypeStruct((b, s, DIFF_W), BF16),
        scratch_shapes=[pltpu.VMEM((2, nblk, blk, LANES), BF16),
                        pltpu.VMEM((nblk, LANES, blk), BF16),
                        pltpu.VMEM((2, LANES, blk), F32)],
        compiler_params=_cparams(("parallel", "parallel", "arbitrary"), 40),
        name="diff_attn",
    )(_alibi_slopes(DIFF_HEADS), qkv, qkv, qkv,
      lq1.reshape(1, -1), lk1.reshape(1, -1), lq2.reshape(1, -1), lk2.reshape(1, -1), subg.reshape(1, -1))


def _dil_kernel(slopes_ref, *refs):
    ngrp = len(DIL_GROUPS)
    in_refs, (o_ref, qf_ref, kf_ref, vf_ref, og_ref, lg_ref) = refs[:5 * ngrp], refs[5 * ngrp:]
    ch, blk = DIL_CHUNK, DIL_BLK
    c = pl.program_id(1)
    slope = slopes_ref[pl.program_id(2)]
    scale = DIL_HD ** -0.5

    qi = lax.broadcasted_iota(jnp.int32, (blk, 2 * blk), 0)
    ki = lax.broadcasted_iota(jnp.int32, (blk, 2 * blk), 1)
    step = blk + qi - ki
    bias = jnp.where((step >= 0) & (step <= blk), -slope * step.astype(F32), NEG)
    seq_start = jnp.where(ki < blk, jnp.where(c == 0, NEG, 0.0), 0.0)

    for g, (window, dil) in enumerate(DIL_GROUPS):
        q_ref, k_ref, v_ref, kp_ref, vp_ref = in_refs[5 * g:5 * g + 5]
        span = blk * dil
        qf_ref[...] = q_ref[...].astype(F32)
        kf_ref[pl.ds(ch, ch), :] = k_ref[...].astype(F32)
        vf_ref[pl.ds(ch, ch), :] = v_ref[...].astype(F32)
        kf_ref[pl.ds(ch - span, span), :] = kp_ref[...].astype(F32)
        vf_ref[pl.ds(ch - span, span), :] = vp_ref[...].astype(F32)

        def unit(uidx, carry, g=g, dil=dil, span=span):
            n_local = uidx // dil
            r = uidx % dil
            q0 = n_local * span + r
            k0 = ch - span + q0
            q = qf_ref[pl.ds(q0, blk, stride=dil), :].astype(BF16)
            kk = kf_ref[pl.ds(k0, 2 * blk, stride=dil), :].astype(BF16)
            vv = vf_ref[pl.ds(k0, 2 * blk, stride=dil), :].astype(BF16)
            s = lax.dot_general(q, kk, (((1,), (1,)), ((), ())), preferred_element_type=F32) * scale
            s = s + bias
            s = s + jnp.where(n_local == 0, seq_start, jnp.zeros_like(seq_start))
            m = jnp.max(s, axis=-1, keepdims=True)
            p = jnp.exp(s - m)
            den = jnp.sum(p, axis=-1, keepdims=True)
            o = jnp.dot(p.astype(BF16), vv, preferred_element_type=F32) * (1.0 / den)
            lse = m + jnp.log(den)
            og_ref[g, pl.ds(q0, blk, stride=dil), :] = o
            lg_ref[g, pl.ds(q0, blk, stride=dil), :] = jnp.broadcast_to(lse, (blk, LANES))
            return carry

        lax.fori_loop(0, ch // blk, unit, 0)

    rows = 256
    for r0 in range(0, ch, rows):
        ls = [lg_ref[g, pl.ds(r0, rows), :] for g in range(ngrp)]
        top = functools.reduce(jnp.maximum, ls)
        ws = [jnp.exp(x - top) for x in ls]
        num = sum(w * og_ref[g, pl.ds(r0, rows), :] for g, w in enumerate(ws))
        o_ref[pl.ds(r0, rows), :] = (num * (1.0 / sum(ws))).astype(o_ref.dtype)


def _dil_attn(qkv):
    b, s, _ = qkv.shape
    ch, blk = DIL_CHUNK, DIL_BLK
    in_specs = [pl.BlockSpec(memory_space=pltpu.SMEM)]
    args = [_alibi_slopes(DIL_HEADS)]
    for g, (window, dil) in enumerate(DIL_GROUPS):
        span = blk * dil
        per = ch // span
        col = g * DIL_HEADS

        def cur(base, col=col):
            return pl.BlockSpec((None, ch, LANES), lambda bi, c, h: (bi, c, base + col + h))

        def prev(base, col=col, span=span, per=per):
            return pl.BlockSpec((None, span, LANES),
                                lambda bi, c, h: (bi, jnp.maximum(c * per - 1, 0), base + col + h))

        in_specs += [cur(COLBLK_SQ), cur(COLBLK_SK), cur(COLBLK_SV), prev(COLBLK_SK), prev(COLBLK_SV)]
        args += [qkv] * 5
    return pl.pallas_call(
        _dil_kernel,
        grid=(b, s // ch, DIL_HEADS),
        in_specs=in_specs,
        out_specs=pl.BlockSpec((None, ch, LANES), lambda bi, c, h: (bi, c, h)),
        out_shape=jax.ShapeDtypeStruct((b, s, DIL_OUT), BF16),
        scratch_shapes=[pltpu.VMEM((ch, LANES), F32),
                        pltpu.VMEM((2 * ch, LANES), F32),
                        pltpu.VMEM((2 * ch, LANES), F32),
                        pltpu.VMEM((len(DIL_GROUPS), ch, LANES), F32),
                        pltpu.VMEM((len(DIL_GROUPS), ch, LANES), F32)],
        compiler_params=_cparams(("parallel", "parallel", "parallel"), 48),
        name="dil_attn",
    )(*args)


def _mix_kernel(od_ref, ol_ref, gd_ref, gl_ref, x_ref, wbd_ref, wbl_ref, wo_ref, o_ref):
    yd = jnp.dot(od_ref[...], wbd_ref[...], preferred_element_type=F32)
    yl = jnp.dot(ol_ref[...], wbl_ref[...], preferred_element_type=F32)
    y = gd_ref[...] * yd + gl_ref[...] * yl
    o_ref[...] = x_ref[...] + jnp.dot(y.astype(BF16), wo_ref[...], preferred_element_type=F32)


def _mix_out(od, ol, gates, x2d, wbd, wbl, wo, *, rb):
    t, d = x2d.shape
    full = lambda a: pl.BlockSpec(a.shape, lambda i: (0, 0))
    return pl.pallas_call(
        _mix_kernel,
        grid=(t // rb,),
        in_specs=[pl.BlockSpec((rb, DIFF_W), lambda i: (i, 0)),
                  pl.BlockSpec((rb, DIL_OUT), lambda i: (i, 0)),
                  pl.BlockSpec((rb, d), lambda i: (i, 0)),
                  pl.BlockSpec((rb, d), lambda i: (i, 1)),
                  pl.BlockSpec((rb, d), lambda i: (i, 0)),
                  full(wbd), full(wbl), full(wo)],
        out_specs=pl.BlockSpec((rb, d), lambda i: (i, 0)),
        out_shape=jax.ShapeDtypeStruct((t, d), F32),
        compiler_params=_cparams(("parallel",), 48),
        name="mix_out",
    )(od, ol, gates, gates, x2d, wbd, wbl, wo)


def _memkv_kernel(mem_ref, g_ref, w_ref, kt_ref, v_ref):
    d = mem_ref.shape[-1]
    mn = _rms(mem_ref[...], g_ref[...]).astype(BF16)
    kv = jnp.dot(mn, w_ref[...], preferred_element_type=F32)
    kt_ref[...] = kv[:, :d].T.astype(BF16)
    v_ref[...] = kv[:, d:].astype(BF16)


def _mem_kv(mem, g, w):
    b, n, d = mem.shape
    return pl.pallas_call(
        _memkv_kernel,
        grid=(b,),
        in_specs=[pl.BlockSpec((None, n, d), lambda bi: (bi, 0, 0)),
                  pl.BlockSpec((1, d), lambda bi: (0, 0)),
                  pl.BlockSpec((d, 2 * d), lambda bi: (0, 0))],
        out_specs=[pl.BlockSpec((None, d, n), lambda bi: (bi, 0, 0)),
                   pl.BlockSpec((None, n, d), lambda bi: (bi, 0, 0))],
        out_shape=[jax.ShapeDtypeStruct((b, d, n), BF16), jax.ShapeDtypeStruct((b, n, d), BF16)],
        compiler_params=_cparams(("parallel",), 32),
        name="mem_kv",
    )(mem, g.reshape(1, d), w)


def _cross_kernel(x_ref, g_ref, wq_ref, kt_ref, v_ref, wo_ref, o_ref):
    d = x_ref.shape[-1]
    hd = d // X_HEADS
    x = x_ref[...]
    hq = _rms(x, g_ref[...]).astype(BF16)
    q = (jnp.dot(hq, wq_ref[...], preferred_element_type=F32) * (hd ** -0.5)).astype(BF16)
    outs = []
    for h in range(X_HEADS):
        sl = slice(h * hd, (h + 1) * hd)
        s = jnp.dot(q[:, sl], kt_ref[sl, :], preferred_element_type=F32)
        m = jnp.max(s, axis=-1, keepdims=True)
        p = jnp.exp(s - m)
        den = jnp.sum(p, axis=-1, keepdims=True)
        outs.append(jnp.dot(p.astype(BF16), v_ref[:, sl], preferred_element_type=F32) * (1.0 / den))
    o = jnp.concatenate(outs, axis=-1).astype(BF16)
    o_ref[...] = x + jnp.dot(o, wo_ref[...], preferred_element_type=F32)


def _cross_attn(x1, g, wq, kt, v, wo, *, rb):
    b, s, d = x1.shape
    n = v.shape[1]
    return pl.pallas_call(
        _cross_kernel,
        grid=(b, s // rb),
        in_specs=[pl.BlockSpec((None, rb, d), lambda bi, i: (bi, i, 0)),
                  pl.BlockSpec((1, d), lambda bi, i: (0, 0)),
                  pl.BlockSpec((d, d), lambda bi, i: (0, 0)),
                  pl.BlockSpec((None, d, n), lambda bi, i: (bi, 0, 0)),
                  pl.BlockSpec((None, n, d), lambda bi, i: (bi, 0, 0)),
                  pl.BlockSpec((d, d), lambda bi, i: (0, 0))],
        out_specs=pl.BlockSpec((None, rb, d), lambda bi, i: (bi, i, 0)),
        out_shape=jax.ShapeDtypeStruct((b, s, d), F32),
        compiler_params=_cparams(("parallel", "parallel"), 48),
        name="cross_attn",
    )(x1, g.reshape(1, d), wq, kt, v, wo)


def _ffn_kernel(x_ref, g_ref, wa_ref, wb_ref, cw_ref, cb_ref, wd_ref, fg_ref, o_ref,
                h_ref, acc_ref, abuf_ref, carry_ref, *, blocks_per_seq, final_norm):
    rb = x_ref.shape[0]
    halo = CONV_HALO
    i = pl.program_id(0)
    c = pl.program_id(1)

    @pl.when(c == 0)
    def _():
        h_ref[...] = _rms(x_ref[...], g_ref[...]).astype(BF16)
        acc_ref[...] = jnp.zeros_like(acc_ref)

    h = h_ref[...]
    a = jnp.dot(h, wa_ref[...], preferred_element_type=F32)
    bgate = jnp.dot(h, wb_ref[...], preferred_element_type=F32)
    first = (i % blocks_per_seq) == 0
    abuf_ref[pl.ds(0, halo), :] = jnp.where(first, jnp.zeros_like(carry_ref[c]), carry_ref[c])
    abuf_ref[pl.ds(halo, rb), :] = a
    carry_ref[c] = a[rb - halo:, :]
    cw = cw_ref[...]
    y = (cw[2:3, :] * a + cw[1:2, :] * abuf_ref[pl.ds(halo - 1, rb), :]
         + cw[0:1, :] * abuf_ref[pl.ds(halo - 2, rb), :] + cb_ref[...])
    act = 0.5 * y * (1.0 + lax.erf(y * (2.0 ** -0.5)))
    acc_ref[...] += jnp.dot((act * bgate).astype(BF16), wd_ref[...], preferred_element_type=F32)

    @pl.when(c == pl.num_programs(1) - 1)
    def _():
        y = x_ref[...] + acc_ref[...]
        o_ref[...] = _rms(y, fg_ref[...]) if final_norm else y


def _ffn(x2d, g, w_up, cw, cb, w_down, fg, *, rb, fc, blocks_per_seq, final_norm):
    t, d = x2d.shape
    ff = w_down.shape[0]
    nfc = ff // fc
    return pl.pallas_call(
        functools.partial(_ffn_kernel, blocks_per_seq=blocks_per_seq, final_norm=final_norm),
        grid=(t // rb, nfc),
        in_specs=[pl.BlockSpec((rb, d), lambda i, c: (i, 0)),
                  pl.BlockSpec((1, d), lambda i, c: (0, 0)),
                  pl.BlockSpec((d, fc), lambda i, c: (0, c)),
                  pl.BlockSpec((d, fc), lambda i, c: (0, nfc + c)),
                  pl.BlockSpec((CONV_W, fc), lambda i, c: (0, c)),
                  pl.BlockSpec((1, fc), lambda i, c: (0, c)),
                  pl.BlockSpec((fc, d), lambda i, c: (c, 0)),
                  pl.BlockSpec((1, d), lambda i, c: (0, 0))],
        out_specs=pl.BlockSpec((rb, d), lambda i, c: (i, 0)),
        out_shape=jax.ShapeDtypeStruct((t, d), F32),
        scratch_shapes=[pltpu.VMEM((rb, d), BF16),
                        pltpu.VMEM((rb, d), F32),
                        pltpu.VMEM((rb + CONV_HALO, fc), F32),
                        pltpu.VMEM((nfc, CONV_HALO, fc), F32)],
        compiler_params=_cparams(("arbitrary", "arbitrary"), 56),
        name="ffn",
    )(x2d, g.reshape(1, d), w_up, w_up, cw, cb.reshape(1, ff), w_down, fg.reshape(1, d))


def kernel(x, mem, mix_norm_g, w_in, b_gate, lam_q1, lam_k1, lam_q2, lam_k2, diff_subln_g,
           w_branch_diff, w_branch_dil, w_out, cross_norm_g, mem_norm_g, w_cq, w_ckv, w_co,
           ffn_norm_g, w_up, conv_w, conv_b, w_down, final_norm_g):
    b, s, d = x.shape
    depth = w_in.shape[0]
    t = b * s
    assert s % DIL_CHUNK == 0 and s % DIFF_BLK == 0 and d == DIFF_W
    xf = x.reshape(t, d)
    for l in range(depth):
        lam_init = 0.8 - 0.6 * math.exp(-0.3 * l)
        w_in_l = w_in[l].astype(BF16)
        qkv = _inproj(xf, mix_norm_g[l], w_in_l[:, :QKV_W], None,
                      out_dtype=BF16, rb=1024, nb=1536, name="inproj_qkv")
        gates = _inproj(xf, mix_norm_g[l], w_in_l[:, QKV_W:], b_gate[l],
                        out_dtype=F32, rb=1024, nb=1024, name="inproj_gate")
        qkv3 = qkv.reshape(b, s, QKV_W)
        o_diff = _diff_attn(qkv3, lam_q1[l], lam_k1[l], lam_q2[l], lam_k2[l], diff_subln_g[l],
                            lam_init=lam_init)
        o_dil = _dil_attn(qkv3)
        x1 = _mix_out(o_diff.reshape(t, DIFF_W), o_dil.reshape(t, DIL_OUT), gates, xf,
                      w_branch_diff[l].astype(BF16), w_branch_dil[l].astype(BF16),
                      w_out[l].astype(BF16), rb=512)
        mk_t, mv = _mem_kv(mem, mem_norm_g[l], w_ckv[l].astype(BF16))
        x2 = _cross_attn(x1.reshape(b, s, d), cross_norm_g[l], w_cq[l].astype(BF16), mk_t, mv,
                         w_co[l].astype(BF16), rb=512)
        xf = _ffn(x2.reshape(t, d), ffn_norm_g[l], w_up[l].astype(BF16), conv_w[l], conv_b[l],
                  w_down[l].astype(BF16), final_norm_g, rb=512, fc=1408, blocks_per_seq=s // 512,
                  final_norm=(l == depth - 1))
    return xf.reshape(b, s, d)
```

```python
import functools
import math

import jax
import jax.numpy as jnp
from jax import lax
from jax.experimental import pallas as pl
from jax.experimental.pallas import tpu as pltpu

F32 = jnp.float32
BF16 = jnp.bfloat16

EPS = 1e-5
NEG = -1e30
LANES = 128

DIFF_HEADS = 8
DIFF_HD = 64
DIFF_W = DIFF_HEADS * 2 * DIFF_HD
DIFF_QB = 256
DIFF_KB = 1024
DIL_GROUPS = ((128, 1), (512, 4), (2048, 16))
DIL_HEADS = 4
DIL_HD = 128
DIL_W = len(DIL_GROUPS) * DIL_HEADS * DIL_HD
DIL_OUT = DIL_HEADS * DIL_HD
DIL_BLK = 128
DIL_CHUNK = 2048
X_HEADS = 4
CONV_W = 3
CONV_HALO = 8

QKV_W = 3 * DIFF_W + 3 * DIL_W
COLBLK_DK = DIFF_W // LANES
COLBLK_DV = 2 * DIFF_W // LANES
COLBLK_SQ = 3 * DIFF_W // LANES
COLBLK_SK = COLBLK_SQ + DIL_W // LANES
COLBLK_SV = COLBLK_SK + DIL_W // LANES


def _alibi_slopes(n):
    return jnp.exp2(-8.0 * jnp.arange(1, n + 1, dtype=F32) / n)


def _cparams(sem, vmem_mb):
    return pltpu.CompilerParams(dimension_semantics=sem, vmem_limit_bytes=vmem_mb * 1024 * 1024)


def _rms(x, g):
    return x * lax.rsqrt(jnp.mean(x * x, axis=-1, keepdims=True) + EPS) * g


def _inproj_kernel(x_ref, g_ref, w_ref, *rest, gate):
    if gate:
        b_ref, o_ref, h_ref = rest
    else:
        o_ref, h_ref = rest

    @pl.when(pl.program_id(1) == 0)
    def _():
        h_ref[...] = _rms(x_ref[...], g_ref[...]).astype(BF16)

    acc = jnp.dot(h_ref[...], w_ref[...], preferred_element_type=F32)
    if gate:
        acc = jax.nn.sigmoid(acc + b_ref[...])
    o_ref[...] = acc.astype(o_ref.dtype)


def _inproj(x2d, g, w, bias, *, out_dtype, rb, nb, name):
    t, d = x2d.shape
    n = w.shape[1]
    gate = bias is not None
    in_specs = [pl.BlockSpec((rb, d), lambda i, j: (i, 0)),
                pl.BlockSpec((1, d), lambda i, j: (0, 0)),
                pl.BlockSpec((d, nb), lambda i, j: (0, j))]
    args = [x2d, g.reshape(1, d), w]
    if gate:
        in_specs.append(pl.BlockSpec((1, nb), lambda i, j: (0, j)))
        args.append(bias.reshape(1, n))
    return pl.pallas_call(
        functools.partial(_inproj_kernel, gate=gate),
        grid=(t // rb, n // nb),
        in_specs=in_specs,
        out_specs=pl.BlockSpec((rb, nb), lambda i, j: (i, j)),
        out_shape=jax.ShapeDtypeStruct((t, n), out_dtype),
        scratch_shapes=[pltpu.VMEM((rb, d), BF16)],
        compiler_params=_cparams(("parallel", "arbitrary"), 48),
        name=name,
    )(*args)


def _diff_kernel(slopes_ref, q_ref, k_ref, v_ref, lq1_ref, lk1_ref, lq2_ref, lk2_ref, subg_ref,
                 o_ref, kaug_ref, vt_ref, acc_ref, ta_ref, tb_ref, *, seq, lam_init):
    qb, kb = DIFF_QB, DIFF_KB
    per = kb // qb
    slope = slopes_ref[pl.program_id(1)]

    lane_k = lax.broadcasted_iota(jnp.int32, (kb, LANES), 1)
    val = slope * lax.broadcasted_iota(jnp.int32, (kb, LANES), 0).astype(F32)
    hi = val.astype(BF16).astype(F32)
    lo = val - hi
    zero_k = jnp.zeros((kb, LANES), F32)

    def build(jb, carry):
        r0 = pl.multiple_of(jb * kb, kb)
        kk = k_ref[pl.ds(r0, kb), :].astype(F32)
        k1 = jnp.where(lane_k < DIFF_HD, kk,
                       jnp.where(lane_k == DIFF_HD, hi, jnp.where(lane_k == DIFF_HD + 1, lo, zero_k)))
        k2 = jnp.where(lane_k >= DIFF_HD, kk,
                       jnp.where(lane_k == 0, hi, jnp.where(lane_k == 1, lo, zero_k)))
        kaug_ref[0, jb] = k1.astype(BF16)
        kaug_ref[1, jb] = k2.astype(BF16)
        vt_ref[jb] = v_ref[pl.ds(r0, kb), :].astype(F32).T.astype(BF16)
        return carry

    lax.fori_loop(0, seq // kb, build, 0)

    lam = (jnp.exp(jnp.sum(lq1_ref[...] * lk1_ref[...], axis=-1, keepdims=True))
           - jnp.exp(jnp.sum(lq2_ref[...] * lk2_ref[...], axis=-1, keepdims=True)) + lam_init)
    lane_q = lax.broadcasted_iota(jnp.int32, (qb, LANES), 1)
    one_q = jnp.ones((qb, LANES), F32)
    zero_q = jnp.zeros((qb, LANES), F32)
    qpos = lax.broadcasted_iota(jnp.int32, (1, qb), 1).astype(F32)
    key_i = lax.broadcasted_iota(jnp.int32, (kb, qb), 0)
    qry_i = lax.broadcasted_iota(jnp.int32, (kb, qb), 1)

    def qblock(qi, carry):
        i0 = pl.multiple_of(qi * qb, qb)
        qq = q_ref[pl.ds(i0, qb), :].astype(F32) * (DIFF_HD ** -0.5)
        q1 = jnp.where(lane_q < DIFF_HD, qq, jnp.where(lane_q < DIFF_HD + 2, one_q, zero_q))
        q2 = jnp.where(lane_q >= DIFF_HD, qq, jnp.where(lane_q < 2, one_q, zero_q))
        q_t = (q1.T.astype(BF16), q2.T.astype(BF16))

        jd = qi // per
        off = (qi % per) * qb

        def scores(k, t_ref):
            jb = jd - k
            mx = []
            for c in range(2):
                t = jnp.dot(kaug_ref[c, jb], q_t[c], preferred_element_type=F32)
                if isinstance(k, int) and k == 0:
                    t = jnp.where(key_i - off <= qry_i, t, NEG)
                t_ref[c] = t
                mx.append(jnp.max(t, axis=0, keepdims=True))
            return tuple(mx)

        def accumulate(k, t_ref, mx, state):
            jb = jd - k
            u = slope * ((off + k * kb).astype(F32) + qpos)
            new_state = []
            for c in range(2):
                m, l = state[c]
                m_new = jnp.maximum(m, mx[c] - u)
                p = jnp.exp(t_ref[c] - (m_new + u))
                alpha = jnp.exp(m - m_new)
                l_new = alpha * l + jnp.sum(p, axis=0, keepdims=True)
                pv = jnp.dot(vt_ref[jb], p.astype(BF16), preferred_element_type=F32)
                acc_ref[c] = alpha * acc_ref[c] + pv
                new_state.append((m_new, l_new))
            return tuple(new_state)

        acc_ref[...] = jnp.zeros_like(acc_ref)
        init = tuple((jnp.full((1, qb), NEG, F32), jnp.zeros((1, qb), F32)) for _ in range(2))
        mx_a = scores(0, ta_ref)

        def pair(pi, carry):
            mx_a, st = carry
            k = 2 * pi
            mx_b = scores(k + 1, tb_ref)
            st = accumulate(k, ta_ref, mx_a, st)
            mx_a = scores(k + 2, ta_ref)
            st = accumulate(k + 1, tb_ref, mx_b, st)
            return mx_a, st

        mx_a, state = lax.fori_loop(0, jd // 2, pair, (mx_a, init))
        k_last = 2 * (jd // 2)

        def odd_tail(mx_a, st):
            mx_b = scores(k_last + 1, tb_ref)
            st = accumulate(k_last, ta_ref, mx_a, st)
            return accumulate(k_last + 1, tb_ref, mx_b, st)

        def even_tail(mx_a, st):
            return accumulate(k_last, ta_ref, mx_a, st)

        (m1, l1), (m2, l2) = lax.cond(jd % 2 == 1, odd_tail, even_tail, mx_a, state)

        o = acc_ref[0] * (1.0 / l1) - lam * (acc_ref[1] * (1.0 / l2))
        y = o * lax.rsqrt(jnp.mean(o * o, axis=0, keepdims=True) + EPS)
        o_ref[pl.ds(i0, qb), :] = (y.T * subg_ref[...] * (1.0 - lam_init)).astype(o_ref.dtype)
        return carry

    lax.fori_loop(0, seq // qb, qblock, 0)


def _diff_attn(qkv, lq1, lk1, lq2, lk2, subg, *, lam_init):
    b, s, _ = qkv.shape
    qb, kb = DIFF_QB, DIFF_KB
    small = lambda w: pl.BlockSpec((1, w), lambda bi, h: (0, 0))
    return pl.pallas_call(
        functools.partial(_diff_kernel, seq=s, lam_init=lam_init),
        grid=(b, DIFF_HEADS),
        in_specs=[pl.BlockSpec(memory_space=pltpu.SMEM),
                  pl.BlockSpec((None, s, LANES), lambda bi, h: (bi, 0, h)),
                  pl.BlockSpec((None, s, LANES), lambda bi, h: (bi, 0, COLBLK_DK + h)),
                  pl.BlockSpec((None, s, LANES), lambda bi, h: (bi, 0, COLBLK_DV + h)),
                  small(DIFF_HD), small(DIFF_HD), small(DIFF_HD), small(DIFF_HD), small(2 * DIFF_HD)],
        out_specs=pl.BlockSpec((None, s, LANES), lambda bi, h: (bi, 0, h)),
        out_shape=jax.ShapeDtypeStruct((b, s, DIFF_W), BF16),
        scratch_shapes=[pltpu.VMEM((2, s // kb, kb, LANES), BF16),
                        pltpu.VMEM((s // kb, LANES, kb), BF16),
                        pltpu.VMEM((2, LANES, qb), F32),
                        pltpu.VMEM((2, kb, qb), F32),
                        pltpu.VMEM((2, kb, qb), F32)],
        compiler_params=_cparams(("parallel", "parallel"), 56),
        name="diff_attn",
    )(_alibi_slopes(DIFF_HEADS), qkv, qkv, qkv,
      lq1.reshape(1, -1), lk1.reshape(1, -1), lq2.reshape(1, -1), lk2.reshape(1, -1), subg.reshape(1, -1))


def _dil_kernel(slopes_ref, *refs):
    ngrp = len(DIL_GROUPS)
    in_refs, (o_ref, qf_ref, kf_ref, vf_ref, og_ref, lg_ref) = refs[:5 * ngrp], refs[5 * ngrp:]
    ch, blk = DIL_CHUNK, DIL_BLK
    c = pl.program_id(1)
    slope = slopes_ref[pl.program_id(2)]
    scale = DIL_HD ** -0.5

    qi = lax.broadcasted_iota(jnp.int32, (blk, 2 * blk), 0)
    ki = lax.broadcasted_iota(jnp.int32, (blk, 2 * blk), 1)
    step = blk + qi - ki
    bias = jnp.where((step >= 0) & (step <= blk), -slope * step.astype(F32), NEG)
    seq_start = jnp.where(ki < blk, jnp.where(c == 0, NEG, 0.0), 0.0)

    for g, (window, dil) in enumerate(DIL_GROUPS):
        q_ref, k_ref, v_ref, kp_ref, vp_ref = in_refs[5 * g:5 * g + 5]
        span = blk * dil
        qf_ref[...] = q_ref[...].astype(F32)
        kf_ref[pl.ds(ch, ch), :] = k_ref[...].astype(F32)
        vf_ref[pl.ds(ch, ch), :] = v_ref[...].astype(F32)
        kf_ref[pl.ds(ch - span, span), :] = kp_ref[...].astype(F32)
        vf_ref[pl.ds(ch - span, span), :] = vp_ref[...].astype(F32)

        def unit(uidx, carry, g=g, dil=dil, span=span):
            n_local = uidx // dil
            r = uidx % dil
            q0 = n_local * span + r
            k0 = ch - span + q0
            q = qf_ref[pl.ds(q0, blk, stride=dil), :].astype(BF16)
            kk = kf_ref[pl.ds(k0, 2 * blk, stride=dil), :].astype(BF16)
            vv = vf_ref[pl.ds(k0, 2 * blk, stride=dil), :].astype(BF16)
            s = lax.dot_general(q, kk, (((1,), (1,)), ((), ())), preferred_element_type=F32) * scale
            s = s + bias
            s = s + jnp.where(n_local == 0, seq_start, jnp.zeros_like(seq_start))
            m = jnp.max(s, axis=-1, keepdims=True)
            p = jnp.exp(s - m)
            den = jnp.sum(p, axis=-1, keepdims=True)
            o = jnp.dot(p.astype(BF16), vv, preferred_element_type=F32) * (1.0 / den)
            lse = m + jnp.log(den)
            og_ref[g, pl.ds(q0, blk, stride=dil), :] = o
            lg_ref[g, pl.ds(q0, blk, stride=dil), :] = jnp.broadcast_to(lse, (blk, LANES))
            return carry

        lax.fori_loop(0, ch // blk, unit, 0, unroll=4)

    rows = 256
    for r0 in range(0, ch, rows):
        ls = [lg_ref[g, pl.ds(r0, rows), :] for g in range(ngrp)]
        top = functools.reduce(jnp.maximum, ls)
        ws = [jnp.exp(x - top) for x in ls]
        num = sum(w * og_ref[g, pl.ds(r0, rows), :] for g, w in enumerate(ws))
        o_ref[pl.ds(r0, rows), :] = (num * (1.0 / sum(ws))).astype(o_ref.dtype)


def _dil_attn(qkv):
    b, s, _ = qkv.shape
    ch, blk = DIL_CHUNK, DIL_BLK
    in_specs = [pl.BlockSpec(memory_space=pltpu.SMEM)]
    args = [_alibi_slopes(DIL_HEADS)]
    for g, (window, dil) in enumerate(DIL_GROUPS):
        span = blk * dil
        per = ch // span
        col = g * DIL_HEADS

        def cur(base, col=col):
            return pl.BlockSpec((None, ch, LANES), lambda bi, c, h: (bi, c, base + col + h))

        def prev(base, col=col, span=span, per=per):
            return pl.BlockSpec((None, span, LANES),
                                lambda bi, c, h: (bi, jnp.maximum(c * per - 1, 0), base + col + h))

        in_specs += [cur(COLBLK_SQ), cur(COLBLK_SK), cur(COLBLK_SV), prev(COLBLK_SK), prev(COLBLK_SV)]
        args += [qkv] * 5
    return pl.pallas_call(
        _dil_kernel,
        grid=(b, s // ch, DIL_HEADS),
        in_specs=in_specs,
        out_specs=pl.BlockSpec((None, ch, LANES), lambda bi, c, h: (bi, c, h)),
        out_shape=jax.ShapeDtypeStruct((b, s, DIL_OUT), BF16),
        scratch_shapes=[pltpu.VMEM((ch, LANES), F32),
                        pltpu.VMEM((2 * ch, LANES), F32),
                        pltpu.VMEM((2 * ch, LANES), F32),
                        pltpu.VMEM((len(DIL_GROUPS), ch, LANES), F32),
                        pltpu.VMEM((len(DIL_GROUPS), ch, LANES), F32)],
        compiler_params=_cparams(("parallel", "parallel", "parallel"), 48),
        name="dil_attn",
    )(*args)


def _mix_kernel(od_ref, ol_ref, gd_ref, gl_ref, x_ref, wbd_ref, wbl_ref, wo_ref, o_ref):
    yd = jnp.dot(od_ref[...], wbd_ref[...], preferred_element_type=F32)
    yl = jnp.dot(ol_ref[...], wbl_ref[...], preferred_element_type=F32)
    y = gd_ref[...] * yd + gl_ref[...] * yl
    o_ref[...] = x_ref[...] + jnp.dot(y.astype(BF16), wo_ref[...], preferred_element_type=F32)


def _mix_out(od, ol, gates, x2d, wbd, wbl, wo, *, rb):
    t, d = x2d.shape
    full = lambda a: pl.BlockSpec(a.shape, lambda i: (0, 0))
    return pl.pallas_call(
        _mix_kernel,
        grid=(t // rb,),
        in_specs=[pl.BlockSpec((rb, DIFF_W), lambda i: (i, 0)),
                  pl.BlockSpec((rb, DIL_OUT), lambda i: (i, 0)),
                  pl.BlockSpec((rb, d), lambda i: (i, 0)),
                  pl.BlockSpec((rb, d), lambda i: (i, 1)),
                  pl.BlockSpec((rb, d), lambda i: (i, 0)),
                  full(wbd), full(wbl), full(wo)],
        out_specs=pl.BlockSpec((rb, d), lambda i: (i, 0)),
        out_shape=jax.ShapeDtypeStruct((t, d), F32),
        compiler_params=_cparams(("parallel",), 48),
        name="mix_out",
    )(od, ol, gates, gates, x2d, wbd, wbl, wo)


def _memkv_kernel(mem_ref, g_ref, w_ref, kt_ref, v_ref):
    d = mem_ref.shape[-1]
    mn = _rms(mem_ref[...], g_ref[...]).astype(BF16)
    kv = jnp.dot(mn, w_ref[...], preferred_element_type=F32)
    kt_ref[...] = kv[:, :d].T.astype(BF16)
    v_ref[...] = kv[:, d:].astype(BF16)


def _mem_kv(mem, g, w):
    b, n, d = mem.shape
    return pl.pallas_call(
        _memkv_kernel,
        grid=(b,),
        in_specs=[pl.BlockSpec((None, n, d), lambda bi: (bi, 0, 0)),
                  pl.BlockSpec((1, d), lambda bi: (0, 0)),
                  pl.BlockSpec((d, 2 * d), lambda bi: (0, 0))],
        out_specs=[pl.BlockSpec((None, d, n), lambda bi: (bi, 0, 0)),
                   pl.BlockSpec((None, n, d), lambda bi: (bi, 0, 0))],
        out_shape=[jax.ShapeDtypeStruct((b, d, n), BF16), jax.ShapeDtypeStruct((b, n, d), BF16)],
        compiler_params=_cparams(("parallel",), 32),
        name="mem_kv",
    )(mem, g.reshape(1, d), w)


def _cross_kernel(x_ref, g_ref, wq_ref, kt_ref, v_ref, wo_ref, o_ref):
    d = x_ref.shape[-1]
    hd = d // X_HEADS
    x = x_ref[...]
    hq = _rms(x, g_ref[...]).astype(BF16)
    q = (jnp.dot(hq, wq_ref[...], preferred_element_type=F32) * (hd ** -0.5)).astype(BF16)
    outs = []
    for h in range(X_HEADS):
        sl = slice(h * hd, (h + 1) * hd)
        s = jnp.dot(q[:, sl], kt_ref[sl, :], preferred_element_type=F32)
        m = jnp.max(s, axis=-1, keepdims=True)
        p = jnp.exp(s - m)
        den = jnp.sum(p, axis=-1, keepdims=True)
        outs.append(jnp.dot(p.astype(BF16), v_ref[:, sl], preferred_element_type=F32) * (1.0 / den))
    o = jnp.concatenate(outs, axis=-1).astype(BF16)
    o_ref[...] = x + jnp.dot(o, wo_ref[...], preferred_element_type=F32)


def _cross_attn(x1, g, wq, kt, v, wo, *, rb):
    b, s, d = x1.shape
    n = v.shape[1]
    return pl.pallas_call(
        _cross_kernel,
        grid=(b, s // rb),
        in_specs=[pl.BlockSpec((None, rb, d), lambda bi, i: (bi, i, 0)),
                  pl.BlockSpec((1, d), lambda bi, i: (0, 0)),
                  pl.BlockSpec((d, d), lambda bi, i: (0, 0)),
                  pl.BlockSpec((None, d, n), lambda bi, i: (bi, 0, 0)),
                  pl.BlockSpec((None, n, d), lambda bi, i: (bi, 0, 0)),
                  pl.BlockSpec((d, d), lambda bi, i: (0, 0))],
        out_specs=pl.BlockSpec((None, rb, d), lambda bi, i: (bi, i, 0)),
        out_shape=jax.ShapeDtypeStruct((b, s, d), F32),
        compiler_params=_cparams(("parallel", "parallel"), 48),
        name="cross_attn",
    )(x1, g.reshape(1, d), wq, kt, v, wo)


def _ffn_kernel(x_ref, g_ref, wa_ref, wb_ref, cw_ref, cb_ref, wd_ref, fg_ref, o_ref,
                h_ref, acc_ref, abuf_ref, carry_ref, *, blocks_per_seq, final_norm):
    rb = x_ref.shape[0]
    halo = CONV_HALO
    i = pl.program_id(0)
    c = pl.program_id(1)

    @pl.when(c == 0)
    def _():
        h_ref[...] = _rms(x_ref[...], g_ref[...]).astype(BF16)
        acc_ref[...] = jnp.zeros_like(acc_ref)

    h = h_ref[...]
    a = jnp.dot(h, wa_ref[...], preferred_element_type=F32)
    bgate = jnp.dot(h, wb_ref[...], preferred_element_type=F32)
    first = (i % blocks_per_seq) == 0
    abuf_ref[pl.ds(0, halo), :] = jnp.where(first, jnp.zeros_like(carry_ref[c]), carry_ref[c])
    abuf_ref[pl.ds(halo, rb), :] = a
    carry_ref[c] = a[rb - halo:, :]
    cw = cw_ref[...]
    y = (cw[2:3, :] * a + cw[1:2, :] * abuf_ref[pl.ds(halo - 1, rb), :]
         + cw[0:1, :] * abuf_ref[pl.ds(halo - 2, rb), :] + cb_ref[...])
    act = 0.5 * y * (1.0 + lax.erf(y * (2.0 ** -0.5)))
    acc_ref[...] += jnp.dot((act * bgate).astype(BF16), wd_ref[...], preferred_element_type=F32)

    @pl.when(c == pl.num_programs(1) - 1)
    def _():
        y = x_ref[...] + acc_ref[...]
        o_ref[...] = _rms(y, fg_ref[...]) if final_norm else y


def _ffn(x2d, g, w_up, cw, cb, w_down, fg, *, rb, fc, blocks_per_seq, final_norm):
    t, d = x2d.shape
    ff = w_down.shape[0]
    nfc = ff // fc
    return pl.pallas_call(
        functools.partial(_ffn_kernel, blocks_per_seq=blocks_per_seq, final_norm=final_norm),
        grid=(t // rb, nfc),
        in_specs=[pl.BlockSpec((rb, d), lambda i, c: (i, 0)),
                  pl.BlockSpec((1, d), lambda i, c: (0, 0)),
                  pl.BlockSpec((d, fc), lambda i, c: (0, c)),
                  pl.BlockSpec((d, fc), lambda i, c: (0, nfc + c)),
                  pl.BlockSpec((CONV_W, fc), lambda i, c: (0, c)),
                  pl.BlockSpec((1, fc), lambda i, c: (0, c)),
                  pl.BlockSpec((fc, d), lambda i, c: (c, 0)),
                  pl.BlockSpec((1, d), lambda i, c: (0, 0))],
        out_specs=pl.BlockSpec((rb, d), lambda i, c: (i, 0)),
        out_shape=jax.ShapeDtypeStruct((t, d), F32),
        scratch_shapes=[pltpu.VMEM((rb, d), BF16),
                        pltpu.VMEM((rb, d), F32),
                        pltpu.VMEM((rb + CONV_HALO, fc), F32),
                        pltpu.VMEM((nfc, CONV_HALO, fc), F32)],
        compiler_params=_cparams(("arbitrary", "arbitrary"), 56),
        name="ffn",
    )(x2d, g.reshape(1, d), w_up, w_up, cw, cb.reshape(1, ff), w_down, fg.reshape(1, d))


def kernel(x, mem, mix_norm_g, w_in, b_gate, lam_q1, lam_k1, lam_q2, lam_k2, diff_subln_g,
           w_branch_diff, w_branch_dil, w_out, cross_norm_g, mem_norm_g, w_cq, w_ckv, w_co,
           ffn_norm_g, w_up, conv_w, conv_b, w_down, final_norm_g):
    b, s, d = x.shape
    depth = w_in.shape[0]
    t = b * s
    assert s % DIL_CHUNK == 0 and s % DIFF_KB == 0 and DIFF_KB % DIFF_QB == 0 and d == DIFF_W
    xf = x.reshape(t, d)
    for l in range(depth):
        lam_init = 0.8 - 0.6 * math.exp(-0.3 * l)
        w_in_l = w_in[l].astype(BF16)
        qkv = _inproj(xf, mix_norm_g[l], w_in_l[:, :QKV_W], None,
                      out_dtype=BF16, rb=1024, nb=1536, name="inproj_qkv")
        gates = _inproj(xf, mix_norm_g[l], w_in_l[:, QKV_W:], b_gate[l],
                        out_dtype=F32, rb=1024, nb=1024, name="inproj_gate")
        qkv3 = qkv.reshape(b, s, QKV_W)
        o_diff = _diff_attn(qkv3, lam_q1[l], lam_k1[l], lam_q2[l], lam_k2[l], diff_subln_g[l],
                            lam_init=lam_init)
        o_dil = _dil_attn(qkv3)
        x1 = _mix_out(o_diff.reshape(t, DIFF_W), o_dil.reshape(t, DIL_OUT), gates, xf,
                      w_branch_diff[l].astype(BF16), w_branch_dil[l].astype(BF16),
                      w_out[l].astype(BF16), rb=512)
        mk_t, mv = _mem_kv(mem, mem_norm_g[l], w_ckv[l].astype(BF16))
        x2 = _cross_attn(x1.reshape(b, s, d), cross_norm_g[l], w_cq[l].astype(BF16), mk_t, mv,
                         w_co[l].astype(BF16), rb=512)
        xf = _ffn(x2.reshape(t, d), ffn_norm_g[l], w_up[l].astype(BF16), conv_w[l], conv_b[l],
                  w_down[l].astype(BF16), final_norm_g, rb=512, fc=1408, blocks_per_seq=s // 512,
                  final_norm=(l == depth - 1))
    return xf.reshape(b, s, d)
```

```python
import functools
import math

import jax
import jax.numpy as jnp
from jax import lax
from jax.experimental import pallas as pl
from jax.experimental.pallas import tpu as pltpu

F32 = jnp.float32
BF16 = jnp.bfloat16

EPS = 1e-5
NEG = -1e30
LOG2E = math.log2(math.e)
LANES = 128

DIFF_HEADS = 8
DIFF_HD = 64
DIFF_W = DIFF_HEADS * 2 * DIFF_HD
DIFF_QB = 256
DIFF_KB = 1024
DIL_GROUPS = ((128, 1), (512, 4), (2048, 16))
DIL_HEADS = 4
DIL_HD = 128
DIL_W = len(DIL_GROUPS) * DIL_HEADS * DIL_HD
DIL_OUT = DIL_HEADS * DIL_HD
DIL_BLK = 128
DIL_CHUNK = 2048
X_HEADS = 4
CONV_W = 3
CONV_HALO = 8

QKV_W = 3 * DIFF_W + 3 * DIL_W
COLBLK_DK = DIFF_W // LANES
COLBLK_DV = 2 * DIFF_W // LANES
COLBLK_SQ = 3 * DIFF_W // LANES
COLBLK_SK = COLBLK_SQ + DIL_W // LANES
COLBLK_SV = COLBLK_SK + DIL_W // LANES


def _alibi_slopes(n):
    return jnp.exp2(-8.0 * jnp.arange(1, n + 1, dtype=F32) / n)


def _cparams(sem, vmem_mb):
    return pltpu.CompilerParams(dimension_semantics=sem, vmem_limit_bytes=vmem_mb * 1024 * 1024)


def _rms(x, g):
    return x * lax.rsqrt(jnp.mean(x * x, axis=-1, keepdims=True) + EPS) * g


def _inproj_kernel(x_ref, g_ref, w_ref, aux_ref, o_ref, h_ref, *, gate):
    @pl.when(pl.program_id(1) == 0)
    def _():
        h_ref[...] = _rms(x_ref[...], g_ref[...]).astype(BF16)

    acc = jnp.dot(h_ref[...], w_ref[...], preferred_element_type=F32)
    acc = jax.nn.sigmoid(acc + aux_ref[...]) if gate else acc * aux_ref[...]
    o_ref[...] = acc.astype(o_ref.dtype)


def _inproj(x2d, g, w, aux, *, gate, out_dtype, rb, nb, name):
    t, d = x2d.shape
    n = w.shape[1]
    in_specs = [pl.BlockSpec((rb, d), lambda i, j: (i, 0)),
                pl.BlockSpec((1, d), lambda i, j: (0, 0)),
                pl.BlockSpec((d, nb), lambda i, j: (0, j)),
                pl.BlockSpec((1, nb), lambda i, j: (0, j))]
    args = [x2d, g.reshape(1, d), w, aux.reshape(1, n)]
    return pl.pallas_call(
        functools.partial(_inproj_kernel, gate=gate),
        grid=(t // rb, n // nb),
        in_specs=in_specs,
        out_specs=pl.BlockSpec((rb, nb), lambda i, j: (i, j)),
        out_shape=jax.ShapeDtypeStruct((t, n), out_dtype),
        scratch_shapes=[pltpu.VMEM((rb, d), BF16)],
        compiler_params=_cparams(("parallel", "arbitrary"), 48),
        name=name,
    )(*args)


def _diff_kernel(slopes_ref, q_ref, k_ref, v_ref, lq1_ref, lk1_ref, lq2_ref, lk2_ref, subg_ref,
                 o_ref, kaug_ref, vt_ref, acc_ref, ta_ref, tb_ref, *, seq, lam_init):
    qb, kb = DIFF_QB, DIFF_KB
    per = kb // qb
    slope = slopes_ref[pl.program_id(1)] * LOG2E

    lane_k = lax.broadcasted_iota(jnp.int32, (kb, LANES), 1)
    val = slope * lax.broadcasted_iota(jnp.int32, (kb, LANES), 0).astype(F32)
    hi = val.astype(BF16).astype(F32)
    mid = (val - hi).astype(BF16).astype(F32)
    lo = val - hi - mid
    zero_k = jnp.zeros((kb, LANES), F32)

    def bias_cols(first):
        return jnp.where(lane_k == first, hi,
                         jnp.where(lane_k == first + 1, mid, jnp.where(lane_k == first + 2, lo, zero_k)))

    bias1, bias2 = bias_cols(DIFF_HD), bias_cols(0)

    def build(jb, carry):
        r0 = pl.multiple_of(jb * kb, kb)
        kk = k_ref[pl.ds(r0, kb), :].astype(F32)
        k1 = jnp.where(lane_k < DIFF_HD, kk, bias1)
        k2 = jnp.where(lane_k >= DIFF_HD, kk, bias2)
        kaug_ref[0, jb] = k1.astype(BF16)
        kaug_ref[1, jb] = k2.astype(BF16)
        vt_ref[jb] = v_ref[pl.ds(r0, kb), :].astype(F32).T.astype(BF16)
        return carry

    lax.fori_loop(0, seq // kb, build, 0)

    lam = (jnp.exp(jnp.sum(lq1_ref[...] * lk1_ref[...], axis=-1, keepdims=True))
           - jnp.exp(jnp.sum(lq2_ref[...] * lk2_ref[...], axis=-1, keepdims=True)) + lam_init)
    lane_q = lax.broadcasted_iota(jnp.int32, (qb, LANES), 1)
    one_q = jnp.ones((qb, LANES), F32)
    zero_q = jnp.zeros((qb, LANES), F32)
    qpos = lax.broadcasted_iota(jnp.int32, (1, qb), 1).astype(F32)
    causal = (lax.broadcasted_iota(jnp.int32, (qb, qb), 0)
              <= lax.broadcasted_iota(jnp.int32, (qb, qb), 1))
    bufs = (ta_ref, tb_ref)

    def scores(q_t, jb, nk, t_ref, diag):
        mx = []
        for c in range(2):
            t = jnp.dot(kaug_ref[c, jb, pl.ds(0, nk), :], q_t[c], preferred_element_type=F32)
            if diag:
                tri = jnp.where(causal, t[nk - qb:, :], NEG)
                t_ref[c, pl.ds(nk - qb, qb), :] = tri
                m = jnp.max(tri, axis=0, keepdims=True)
                if nk > qb:
                    t_ref[c, pl.ds(0, nk - qb), :] = t[:nk - qb, :]
                    m = jnp.maximum(m, jnp.max(t[:nk - qb, :], axis=0, keepdims=True))
            else:
                t_ref[c] = t
                m = jnp.max(t, axis=0, keepdims=True)
            mx.append(m)
        return tuple(mx)

    def accumulate(strip, jb, nk, dist0, t_ref, mx, state):
        u = slope * (jnp.asarray(dist0, F32) + qpos)
        new_state = []
        for c in range(2):
            if state is None:
                m_new = mx[c] - u
            else:
                m, l = state[c]
                m_new = jnp.maximum(m, mx[c] - u)
            p = jnp.exp2(t_ref[c, pl.ds(0, nk), :] - (m_new + u))
            l_new = jnp.sum(p, axis=0, keepdims=True)
            pv = jnp.dot(vt_ref[jb, :, pl.ds(0, nk)], p.astype(BF16), preferred_element_type=F32)
            if state is None:
                acc_ref[strip, c] = pv
            else:
                alpha = jnp.exp2(m - m_new)
                l_new = alpha * l + l_new
                acc_ref[strip, c] = alpha * acc_ref[strip, c] + pv
            new_state.append((m_new, l_new))
        return tuple(new_state)

    def super_block(sb, carry):
        q_t = []
        for s in range(per):
            i0 = pl.multiple_of(sb * kb + s * qb, qb)
            qq = q_ref[pl.ds(i0, qb), :].astype(F32)
            q1 = jnp.where(lane_q < DIFF_HD, qq, jnp.where(lane_q < DIFF_HD + 3, one_q, zero_q))
            q2 = jnp.where(lane_q >= DIFF_HD, qq, jnp.where(lane_q < 3, one_q, zero_q))
            q_t.append((q1.T.astype(BF16), q2.T.astype(BF16)))

        state = []
        mx = scores(q_t[0], sb, qb, bufs[0], True)
        for s in range(per):
            if s + 1 < per:
                mx_next = scores(q_t[s + 1], sb, (s + 2) * qb, bufs[(s + 1) % 2], True)
            else:
                mx_next = scores(q_t[0], jnp.maximum(sb - 1, 0), kb, bufs[per % 2], False)
            state.append(accumulate(s, sb, (s + 1) * qb, s * qb, bufs[s % 2], mx, None))
            mx = mx_next

        def full_tile(n, carry):
            mx, state = carry
            jb = sb - 1 - n
            dist = (n + 1) * kb
            state = list(state)
            for s in range(per):
                if s + 1 < per:
                    mx_next = scores(q_t[s + 1], jb, kb, bufs[(per + s + 1) % 2], False)
                else:
                    mx_next = scores(q_t[0], jnp.maximum(jb - 1, 0), kb, bufs[per % 2], False)
                state[s] = accumulate(s, jb, kb, dist + s * qb, bufs[(per + s) % 2], mx, state[s])
                mx = mx_next
            return mx, tuple(state)

        _, state = lax.fori_loop(0, sb, full_tile, (mx, tuple(state)))

        for s in range(per):
            (m1, l1), (m2, l2) = state[s]
            o = acc_ref[s, 0] * (1.0 / l1) - lam * (acc_ref[s, 1] * (1.0 / l2))
            y = o * lax.rsqrt(jnp.mean(o * o, axis=0, keepdims=True) + EPS)
            i0 = pl.multiple_of(sb * kb + s * qb, qb)
            o_ref[pl.ds(i0, qb), :] = (y.T * subg_ref[...] * (1.0 - lam_init)).astype(o_ref.dtype)
        return carry

    assert per % 2 == 0
    lax.fori_loop(0, seq // kb, super_block, 0)


def _diff_attn(qkv, lq1, lk1, lq2, lk2, subg, *, lam_init):
    b, s, _ = qkv.shape
    qb, kb = DIFF_QB, DIFF_KB
    small = lambda w: pl.BlockSpec((1, w), lambda bi, h: (0, 0))
    return pl.pallas_call(
        functools.partial(_diff_kernel, seq=s, lam_init=lam_init),
        grid=(b, DIFF_HEADS),
        in_specs=[pl.BlockSpec(memory_space=pltpu.SMEM),
                  pl.BlockSpec((None, s, LANES), lambda bi, h: (bi, 0, h)),
                  pl.BlockSpec((None, s, LANES), lambda bi, h: (bi, 0, COLBLK_DK + h)),
                  pl.BlockSpec((None, s, LANES), lambda bi, h: (bi, 0, COLBLK_DV + h)),
                  small(DIFF_HD), small(DIFF_HD), small(DIFF_HD), small(DIFF_HD), small(2 * DIFF_HD)],
        out_specs=pl.BlockSpec((None, s, LANES), lambda bi, h: (bi, 0, h)),
        out_shape=jax.ShapeDtypeStruct((b, s, DIFF_W), BF16),
        scratch_shapes=[pltpu.VMEM((2, s // kb, kb, LANES), BF16),
                        pltpu.VMEM((s // kb, LANES, kb), BF16),
                        pltpu.VMEM((kb // qb, 2, LANES, qb), F32),
                        pltpu.VMEM((2, kb, qb), F32),
                        pltpu.VMEM((2, kb, qb), F32)],
        compiler_params=_cparams(("parallel", "parallel"), 56),
        name="diff_attn",
    )(_alibi_slopes(DIFF_HEADS), qkv, qkv, qkv,
      lq1.reshape(1, -1), lk1.reshape(1, -1), lq2.reshape(1, -1), lk2.reshape(1, -1), subg.reshape(1, -1))


def _dil_kernel(slopes_ref, *refs):
    ngrp = len(DIL_GROUPS)
    in_refs, (o_ref, qf_ref, kf_ref, vf_ref, og_ref, lg_ref) = refs[:5 * ngrp], refs[5 * ngrp:]
    ch, blk = DIL_CHUNK, DIL_BLK
    c = pl.program_id(1)
    slope = slopes_ref[pl.program_id(2)]
    scale = DIL_HD ** -0.5

    qi = lax.broadcasted_iota(jnp.int32, (blk, 2 * blk), 0)
    ki = lax.broadcasted_iota(jnp.int32, (blk, 2 * blk), 1)
    step = blk + qi - ki
    bias = jnp.where((step >= 0) & (step <= blk), -slope * step.astype(F32), NEG)
    seq_start = jnp.where(ki < blk, jnp.where(c == 0, NEG, 0.0), 0.0)

    for g, (window, dil) in enumerate(DIL_GROUPS):
        q_ref, k_ref, v_ref, kp_ref, vp_ref = in_refs[5 * g:5 * g + 5]
        span = blk * dil
        qf_ref[...] = q_ref[...].astype(F32)
        kf_ref[pl.ds(ch, ch), :] = k_ref[...].astype(F32)
        vf_ref[pl.ds(ch, ch), :] = v_ref[...].astype(F32)
        kf_ref[pl.ds(ch - span, span), :] = kp_ref[...].astype(F32)
        vf_ref[pl.ds(ch - span, span), :] = vp_ref[...].astype(F32)

        def unit(uidx, carry, g=g, dil=dil, span=span):
            n_local = uidx // dil
            r = uidx % dil
            q0 = n_local * span + r
            k0 = ch - span + q0
            q = qf_ref[pl.ds(q0, blk, stride=dil), :].astype(BF16)
            kk = kf_ref[pl.ds(k0, 2 * blk, stride=dil), :].astype(BF16)
            vv = vf_ref[pl.ds(k0, 2 * blk, stride=dil), :].astype(BF16)
            s = lax.dot_general(q, kk, (((1,), (1,)), ((), ())), preferred_element_type=F32) * scale
            s = s + bias
            s = s + jnp.where(n_local == 0, seq_start, jnp.zeros_like(seq_start))
            m = jnp.max(s, axis=-1, keepdims=True)
            p = jnp.exp(s - m)
            den = jnp.sum(p, axis=-1, keepdims=True)
            o = jnp.dot(p.astype(BF16), vv, preferred_element_type=F32) * (1.0 / den)
            lse = m + jnp.log(den)
            og_ref[g, pl.ds(q0, blk, stride=dil), :] = o
            lg_ref[g, pl.ds(q0, blk, stride=dil), :] = jnp.broadcast_to(lse, (blk, LANES))
            return carry

        lax.fori_loop(0, ch // blk, unit, 0, unroll=16)

    rows = 256
    for r0 in range(0, ch, rows):
        ls = [lg_ref[g, pl.ds(r0, rows), :] for g in range(ngrp)]
        top = functools.reduce(jnp.maximum, ls)
        ws = [jnp.exp(x - top) for x in ls]
        num = sum(w * og_ref[g, pl.ds(r0, rows), :] for g, w in enumerate(ws))
        o_ref[pl.ds(r0, rows), :] = (num * (1.0 / sum(ws))).astype(o_ref.dtype)


def _dil_attn(qkv):
    b, s, _ = qkv.shape
    ch, blk = DIL_CHUNK, DIL_BLK
    in_specs = [pl.BlockSpec(memory_space=pltpu.SMEM)]
    args = [_alibi_slopes(DIL_HEADS)]
    for g, (window, dil) in enumerate(DIL_GROUPS):
        span = blk * dil
        per = ch // span
        col = g * DIL_HEADS

        def cur(base, col=col):
            return pl.BlockSpec((None, ch, LANES), lambda bi, c, h: (bi, c, base + col + h))

        def prev(base, col=col, span=span, per=per):
            return pl.BlockSpec((None, span, LANES),
                                lambda bi, c, h: (bi, jnp.maximum(c * per - 1, 0), base + col + h))

        in_specs += [cur(COLBLK_SQ), cur(COLBLK_SK), cur(COLBLK_SV), prev(COLBLK_SK), prev(COLBLK_SV)]
        args += [qkv] * 5
    return pl.pallas_call(
        _dil_kernel,
        grid=(b, s // ch, DIL_HEADS),
        in_specs=in_specs,
        out_specs=pl.BlockSpec((None, ch, LANES), lambda bi, c, h: (bi, c, h)),
        out_shape=jax.ShapeDtypeStruct((b, s, DIL_OUT), BF16),
        scratch_shapes=[pltpu.VMEM((ch, LANES), F32),
                        pltpu.VMEM((2 * ch, LANES), F32),
                        pltpu.VMEM((2 * ch, LANES), F32),
                        pltpu.VMEM((len(DIL_GROUPS), ch, LANES), F32),
                        pltpu.VMEM((len(DIL_GROUPS), ch, LANES), F32)],
        compiler_params=_cparams(("parallel", "parallel", "parallel"), 48),
        name="dil_attn",
    )(*args)


def _mix_kernel(od_ref, ol_ref, gd_ref, gl_ref, x_ref, wbd_ref, wbl_ref, wo_ref, o_ref):
    yd = jnp.dot(od_ref[...], wbd_ref[...], preferred_element_type=F32)
    yl = jnp.dot(ol_ref[...], wbl_ref[...], preferred_element_type=F32)
    y = gd_ref[...] * yd + gl_ref[...] * yl
    o_ref[...] = x_ref[...] + jnp.dot(y.astype(BF16), wo_ref[...], preferred_element_type=F32)


def _mix_out(od, ol, gates, x2d, wbd, wbl, wo, *, rb):
    t, d = x2d.shape
    full = lambda a: pl.BlockSpec(a.shape, lambda i: (0, 0))
    return pl.pallas_call(
        _mix_kernel,
        grid=(t // rb,),
        in_specs=[pl.BlockSpec((rb, DIFF_W), lambda i: (i, 0)),
                  pl.BlockSpec((rb, DIL_OUT), lambda i: (i, 0)),
                  pl.BlockSpec((rb, d), lambda i: (i, 0)),
                  pl.BlockSpec((rb, d), lambda i: (i, 1)),
                  pl.BlockSpec((rb, d), lambda i: (i, 0)),
                  full(wbd), full(wbl), full(wo)],
        out_specs=pl.BlockSpec((rb, d), lambda i: (i, 0)),
        out_shape=jax.ShapeDtypeStruct((t, d), F32),
        compiler_params=_cparams(("parallel",), 48),
        name="mix_out",
    )(od, ol, gates, gates, x2d, wbd, wbl, wo)


def _memkv_kernel(mem_ref, g_ref, w_ref, kt_ref, v_ref):
    d = mem_ref.shape[-1]
    mn = _rms(mem_ref[...], g_ref[...]).astype(BF16)
    kv = jnp.dot(mn, w_ref[...], preferred_element_type=F32)
    kt_ref[...] = kv[:, :d].T.astype(BF16)
    v_ref[...] = kv[:, d:].astype(BF16)


def _mem_kv(mem, g, w):
    b, n, d = mem.shape
    return pl.pallas_call(
        _memkv_kernel,
        grid=(b,),
        in_specs=[pl.BlockSpec((None, n, d), lambda bi: (bi, 0, 0)),
                  pl.BlockSpec((1, d), lambda bi: (0, 0)),
                  pl.BlockSpec((d, 2 * d), lambda bi: (0, 0))],
        out_specs=[pl.BlockSpec((None, d, n), lambda bi: (bi, 0, 0)),
                   pl.BlockSpec((None, n, d), lambda bi: (bi, 0, 0))],
        out_shape=[jax.ShapeDtypeStruct((b, d, n), BF16), jax.ShapeDtypeStruct((b, n, d), BF16)],
        compiler_params=_cparams(("parallel",), 32),
        name="mem_kv",
    )(mem, g.reshape(1, d), w)


def _cross_kernel(x_ref, g_ref, wq_ref, kt_ref, v_ref, wo_ref, o_ref):
    d = x_ref.shape[-1]
    hd = d // X_HEADS
    x = x_ref[...]
    hq = _rms(x, g_ref[...]).astype(BF16)
    q = (jnp.dot(hq, wq_ref[...], preferred_element_type=F32) * (hd ** -0.5)).astype(BF16)
    outs = []
    for h in range(X_HEADS):
        sl = slice(h * hd, (h + 1) * hd)
        s = jnp.dot(q[:, sl], kt_ref[sl, :], preferred_element_type=F32)
        m = jnp.max(s, axis=-1, keepdims=True)
        p = jnp.exp(s - m)
        den = jnp.sum(p, axis=-1, keepdims=True)
        outs.append(jnp.dot(p.astype(BF16), v_ref[:, sl], preferred_element_type=F32) * (1.0 / den))
    o = jnp.concatenate(outs, axis=-1).astype(BF16)
    o_ref[...] = x + jnp.dot(o, wo_ref[...], preferred_element_type=F32)


def _cross_attn(x1, g, wq, kt, v, wo, *, rb):
    b, s, d = x1.shape
    n = v.shape[1]
    return pl.pallas_call(
        _cross_kernel,
        grid=(b, s // rb),
        in_specs=[pl.BlockSpec((None, rb, d), lambda bi, i: (bi, i, 0)),
                  pl.BlockSpec((1, d), lambda bi, i: (0, 0)),
                  pl.BlockSpec((d, d), lambda bi, i: (0, 0)),
                  pl.BlockSpec((None, d, n), lambda bi, i: (bi, 0, 0)),
                  pl.BlockSpec((None, n, d), lambda bi, i: (bi, 0, 0)),
                  pl.BlockSpec((d, d), lambda bi, i: (0, 0))],
        out_specs=pl.BlockSpec((None, rb, d), lambda bi, i: (bi, i, 0)),
        out_shape=jax.ShapeDtypeStruct((b, s, d), F32),
        compiler_params=_cparams(("parallel", "parallel"), 48),
        name="cross_attn",
    )(x1, g.reshape(1, d), wq, kt, v, wo)


def _ffn_kernel(x_ref, g_ref, wa_ref, wb_ref, cw_ref, cb_ref, wd_ref, fg_ref, o_ref,
                h_ref, acc_ref, abuf_ref, carry_ref, *, blocks_per_seq, final_norm):
    rb = x_ref.shape[0]
    halo = CONV_HALO
    i = pl.program_id(0)
    c = pl.program_id(1)

    @pl.when(c == 0)
    def _():
        h_ref[...] = _rms(x_ref[...], g_ref[...]).astype(BF16)
        acc_ref[...] = jnp.zeros_like(acc_ref)

    h = h_ref[...]
    a = jnp.dot(h, wa_ref[...], preferred_element_type=F32)
    bgate = jnp.dot(h, wb_ref[...], preferred_element_type=F32)
    first = (i % blocks_per_seq) == 0
    abuf_ref[pl.ds(0, halo), :] = jnp.where(first, jnp.zeros_like(carry_ref[c]), carry_ref[c])
    abuf_ref[pl.ds(halo, rb), :] = a
    carry_ref[c] = a[rb - halo:, :]
    cw = cw_ref[...]
    y = (cw[2:3, :] * a + cw[1:2, :] * abuf_ref[pl.ds(halo - 1, rb), :]
         + cw[0:1, :] * abuf_ref[pl.ds(halo - 2, rb), :] + cb_ref[...])
    act = 0.5 * y * (1.0 + lax.erf(y * (2.0 ** -0.5)))
    acc_ref[...] += jnp.dot((act * bgate).astype(BF16), wd_ref[...], preferred_element_type=F32)

    @pl.when(c == pl.num_programs(1) - 1)
    def _():
        y = x_ref[...] + acc_ref[...]
        o_ref[...] = _rms(y, fg_ref[...]) if final_norm else y


def _ffn(x2d, g, w_up, cw, cb, w_down, fg, *, rb, fc, blocks_per_seq, final_norm):
    t, d = x2d.shape
    ff = w_down.shape[0]
    nfc = ff // fc
    return pl.pallas_call(
        functools.partial(_ffn_kernel, blocks_per_seq=blocks_per_seq, final_norm=final_norm),
        grid=(t // rb, nfc),
        in_specs=[pl.BlockSpec((rb, d), lambda i, c: (i, 0)),
                  pl.BlockSpec((1, d), lambda i, c: (0, 0)),
                  pl.BlockSpec((d, fc), lambda i, c: (0, c)),
                  pl.BlockSpec((d, fc), lambda i, c: (0, nfc + c)),
                  pl.BlockSpec((CONV_W, fc), lambda i, c: (0, c)),
                  pl.BlockSpec((1, fc), lambda i, c: (0, c)),
                  pl.BlockSpec((fc, d), lambda i, c: (c, 0)),
                  pl.BlockSpec((1, d), lambda i, c: (0, 0))],
        out_specs=pl.BlockSpec((rb, d), lambda i, c: (i, 0)),
        out_shape=jax.ShapeDtypeStruct((t, d), F32),
        scratch_shapes=[pltpu.VMEM((rb, d), BF16),
                        pltpu.VMEM((rb, d), F32),
                        pltpu.VMEM((rb + CONV_HALO, fc), F32),
                        pltpu.VMEM((nfc, CONV_HALO, fc), F32)],
        compiler_params=_cparams(("arbitrary", "arbitrary"), 56),
        name="ffn",
    )(x2d, g.reshape(1, d), w_up, w_up, cw, cb.reshape(1, ff), w_down, fg.reshape(1, d))


def kernel(x, mem, mix_norm_g, w_in, b_gate, lam_q1, lam_k1, lam_q2, lam_k2, diff_subln_g,
           w_branch_diff, w_branch_dil, w_out, cross_norm_g, mem_norm_g, w_cq, w_ckv, w_co,
           ffn_norm_g, w_up, conv_w, conv_b, w_down, final_norm_g):
    b, s, d = x.shape
    depth = w_in.shape[0]
    t = b * s
    assert s % DIL_CHUNK == 0 and s % DIFF_KB == 0 and DIFF_KB % DIFF_QB == 0 and d == DIFF_W
    xf = x.reshape(t, d)
    for l in range(depth):
        lam_init = 0.8 - 0.6 * math.exp(-0.3 * l)
        w_in_l = w_in[l].astype(BF16)
        col_scale = jnp.where(jnp.arange(QKV_W) < DIFF_W, LOG2E * DIFF_HD ** -0.5, 1.0).astype(F32)
        qkv = _inproj(xf, mix_norm_g[l], w_in_l[:, :QKV_W], col_scale, gate=False,
                      out_dtype=BF16, rb=1024, nb=1536, name="inproj_qkv")
        gates = _inproj(xf, mix_norm_g[l], w_in_l[:, QKV_W:], b_gate[l], gate=True,
                        out_dtype=F32, rb=1024, nb=1024, name="inproj_gate")
        qkv3 = qkv.reshape(b, s, QKV_W)
        o_diff = _diff_attn(qkv3, lam_q1[l], lam_k1[l], lam_q2[l], lam_k2[l], diff_subln_g[l],
                            lam_init=lam_init)
        o_dil = _dil_attn(qkv3)
        x1 = _mix_out(o_diff.reshape(t, DIFF_W), o_dil.reshape(t, DIL_OUT), gates, xf,
                      w_branch_diff[l].astype(BF16), w_branch_dil[l].astype(BF16),
                      w_out[l].astype(BF16), rb=512)
        mk_t, mv = _mem_kv(mem, mem_norm_g[l], w_ckv[l].astype(BF16))
        x2 = _cross_attn(x1.reshape(b, s, d), cross_norm_g[l], w_cq[l].astype(BF16), mk_t, mv,
                         w_co[l].astype(BF16), rb=512)
        xf = _ffn(x2.reshape(t, d), ffn_norm_g[l], w_up[l].astype(BF16), conv_w[l], conv_b[l],
                  w_down[l].astype(BF16), final_norm_g, rb=512, fc=1408, blocks_per_seq=s // 512,
                  final_norm=(l == depth - 1))
    return xf.reshape(b, s, d)
```

```python
import functools
import math

import jax
import jax.numpy as jnp
from jax import lax
from jax.experimental import pallas as pl
from jax.experimental.pallas import tpu as pltpu

F32 = jnp.float32
BF16 = jnp.bfloat16

EPS = 1e-5
NEG = -1e30
LOG2E = math.log2(math.e)
LANES = 128

DIFF_HEADS = 8
DIFF_HD = 64
DIFF_W = DIFF_HEADS * 2 * DIFF_HD
DIFF_QB = 256
DIFF_KB = 1024
DIFF_LROWS = 16
DIL_GROUPS = ((128, 1), (512, 4), (2048, 16))
DIL_HEADS = 4
DIL_HD = 128
DIL_W = len(DIL_GROUPS) * DIL_HEADS * DIL_HD
DIL_OUT = DIL_HEADS * DIL_HD
DIL_BLK = 128
DIL_CHUNK = 2048
X_HEADS = 4
CONV_W = 3
CONV_HALO = 8
FFN_CHUNK = 1024

QKV_W = 3 * DIFF_W + 3 * DIL_W
INPROJ_NC = 512
COLBLK_DK = DIFF_W // LANES
COLBLK_DV = 2 * DIFF_W // LANES
COLBLK_SQ = 3 * DIFF_W // LANES
COLBLK_SK = COLBLK_SQ + DIL_W // LANES
COLBLK_SV = COLBLK_SK + DIL_W // LANES


def _alibi_slopes(n):
    return jnp.exp2(-8.0 * jnp.arange(1, n + 1, dtype=F32) / n)


def _cparams(sem, vmem_mb):
    return pltpu.CompilerParams(dimension_semantics=sem, vmem_limit_bytes=vmem_mb * 1024 * 1024)


def _rms(x, g):
    return x * lax.rsqrt(jnp.mean(x * x, axis=-1, keepdims=True) + EPS) * g


def _resident(shape):
    return pl.BlockSpec(shape, lambda *_: (0,) * len(shape), pipeline_mode=pl.Buffered(1))


def _inproj_kernel(x_ref, g_ref, w_ref, b_ref, qkv_ref, gate_ref):
    h = _rms(x_ref[...], g_ref[...]).astype(BF16)
    nq, ng, nc = qkv_ref.shape[1], gate_ref.shape[1], INPROJ_NC
    for c0 in range(0, nq, nc):
        acc = jnp.dot(h, w_ref[:, pl.ds(c0, nc)], preferred_element_type=F32)
        if c0 < DIFF_W:
            acc = acc * (LOG2E * DIFF_HD ** -0.5)
        qkv_ref[:, pl.ds(c0, nc)] = acc.astype(qkv_ref.dtype)
    for c0 in range(0, ng, nc):
        acc = jnp.dot(h, w_ref[:, pl.ds(nq + c0, nc)], preferred_element_type=F32)
        gate_ref[:, pl.ds(c0, nc)] = jax.nn.sigmoid(acc + b_ref[:, pl.ds(c0, nc)])


def _inproj(x2d, g, w, b_gate, *, rb):
    t, d = x2d.shape
    ng = b_gate.shape[0]
    nq = w.shape[1] - ng
    assert DIFF_W % INPROJ_NC == 0 and nq % INPROJ_NC == 0 and ng % INPROJ_NC == 0
    return pl.pallas_call(
        _inproj_kernel,
        grid=(t // rb,),
        in_specs=[pl.BlockSpec((rb, d), lambda i: (i, 0)),
                  _resident((1, d)), _resident(w.shape), _resident((1, ng))],
        out_specs=[pl.BlockSpec((rb, nq), lambda i: (i, 0)),
                   pl.BlockSpec((rb, ng), lambda i: (i, 0))],
        out_shape=[jax.ShapeDtypeStruct((t, nq), BF16), jax.ShapeDtypeStruct((t, ng), F32)],
        compiler_params=_cparams(("parallel",), 60),
        name="inproj",
    )(x2d, g.reshape(1, d), w, b_gate.reshape(1, ng))


def _diff_kernel(slopes_ref, q_ref, k_ref, v_ref, lq1_ref, lk1_ref, lq2_ref, lk2_ref, subg_ref,
                 o_ref, kaug_ref, vt_ref, acc_ref, ta_ref, tb_ref, *, seq, lam_init):
    qb, kb = DIFF_QB, DIFF_KB
    per = kb // qb
    slope = slopes_ref[pl.program_id(1)] * LOG2E

    lane_k = lax.broadcasted_iota(jnp.int32, (kb, LANES), 1)
    val = slope * lax.broadcasted_iota(jnp.int32, (kb, LANES), 0).astype(F32)
    hi = val.astype(BF16).astype(F32)
    mid = (val - hi).astype(BF16).astype(F32)
    lo = val - hi - mid
    zero_k = jnp.zeros((kb, LANES), F32)

    def bias_cols(first):
        return jnp.where(lane_k == first, hi,
                         jnp.where(lane_k == first + 1, mid, jnp.where(lane_k == first + 2, lo, zero_k)))

    bias1, bias2 = bias_cols(DIFF_HD), bias_cols(0)
    ones_row = jnp.where(lax.broadcasted_iota(jnp.int32, (DIFF_LROWS, kb), 0) == 0, 1.0, 0.0).astype(BF16)

    def build(jb, carry):
        r0 = pl.multiple_of(jb * kb, kb)
        kk = k_ref[pl.ds(r0, kb), :].astype(F32)
        k1 = jnp.where(lane_k < DIFF_HD, kk, bias1)
        k2 = jnp.where(lane_k >= DIFF_HD, kk, bias2)
        kaug_ref[0, jb] = k1.astype(BF16)
        kaug_ref[1, jb] = k2.astype(BF16)
        vt_ref[jb, pl.ds(0, LANES), :] = v_ref[pl.ds(r0, kb), :].astype(F32).T.astype(BF16)
        vt_ref[jb, pl.ds(LANES, DIFF_LROWS), :] = ones_row
        return carry

    lax.fori_loop(0, seq // kb, build, 0)

    lam = (jnp.exp(jnp.sum(lq1_ref[...] * lk1_ref[...], axis=-1, keepdims=True))
           - jnp.exp(jnp.sum(lq2_ref[...] * lk2_ref[...], axis=-1, keepdims=True)) + lam_init)
    lane_q = lax.broadcasted_iota(jnp.int32, (qb, LANES), 1)
    one_q = jnp.ones((qb, LANES), F32)
    zero_q = jnp.zeros((qb, LANES), F32)
    qpos = lax.broadcasted_iota(jnp.int32, (1, qb), 1).astype(F32)
    causal = (lax.broadcasted_iota(jnp.int32, (qb, qb), 0)
              <= lax.broadcasted_iota(jnp.int32, (qb, qb), 1))
    bufs = (ta_ref, tb_ref)

    def scores(q_t, jb, nk, t_ref, diag):
        mx = []
        for c in range(2):
            t = jnp.dot(kaug_ref[c, jb, pl.ds(0, nk), :], q_t[c], preferred_element_type=F32)
            if diag:
                tri = jnp.where(causal, t[nk - qb:, :], NEG)
                t_ref[c, pl.ds(nk - qb, qb), :] = tri
                m = jnp.max(tri, axis=0, keepdims=True)
                if nk > qb:
                    t_ref[c, pl.ds(0, nk - qb), :] = t[:nk - qb, :]
                    m = jnp.maximum(m, jnp.max(t[:nk - qb, :], axis=0, keepdims=True))
            else:
                t_ref[c] = t
                m = jnp.max(t, axis=0, keepdims=True)
            mx.append(m)
        return tuple(mx)

    def accumulate(strip, jb, nk, dist0, t_ref, mx, state):
        u = slope * (jnp.asarray(dist0, F32) + qpos)
        new_state = []
        for c in range(2):
            m_new = mx[c] - u if state is None else jnp.maximum(state[c], mx[c] - u)
            p = jnp.exp2(t_ref[c, pl.ds(0, nk), :] - (m_new + u))
            pv = jnp.dot(vt_ref[jb, :, pl.ds(0, nk)], p.astype(BF16), preferred_element_type=F32)
            if state is None:
                acc_ref[strip, c] = pv
            else:
                acc_ref[strip, c] = jnp.exp2(state[c] - m_new) * acc_ref[strip, c] + pv
            new_state.append(m_new)
        return tuple(new_state)

    def super_block(sb, carry):
        q_t = []
        for s in range(per):
            i0 = pl.multiple_of(sb * kb + s * qb, qb)
            qq = q_ref[pl.ds(i0, qb), :].astype(F32)
            q1 = jnp.where(lane_q < DIFF_HD, qq, jnp.where(lane_q < DIFF_HD + 3, one_q, zero_q))
            q2 = jnp.where(lane_q >= DIFF_HD, qq, jnp.where(lane_q < 3, one_q, zero_q))
            q_t.append((q1.T.astype(BF16), q2.T.astype(BF16)))

        state = []
        mx = scores(q_t[0], sb, qb, bufs[0], True)
        for s in range(per):
            if s + 1 < per:
                mx_next = scores(q_t[s + 1], sb, (s + 2) * qb, bufs[(s + 1) % 2], True)
            else:
                mx_next = scores(q_t[0], jnp.maximum(sb - 1, 0), kb, bufs[per % 2], False)
            state.append(accumulate(s, sb, (s + 1) * qb, s * qb, bufs[s % 2], mx, None))
            mx = mx_next

        def full_tile(n, carry):
            mx, state = carry
            jb = sb - 1 - n
            dist = (n + 1) * kb
            state = list(state)
            for s in range(per):
                if s + 1 < per:
                    mx_next = scores(q_t[s + 1], jb, kb, bufs[(per + s + 1) % 2], False)
                else:
                    mx_next = scores(q_t[0], jnp.maximum(jb - 1, 0), kb, bufs[per % 2], False)
                state[s] = accumulate(s, jb, kb, dist + s * qb, bufs[(per + s) % 2], mx, state[s])
                mx = mx_next
            return mx, tuple(state)

        _, state = lax.fori_loop(0, sb, full_tile, (mx, tuple(state)))

        for s in range(per):
            l1 = acc_ref[s, 0, pl.ds(LANES, 1), :]
            l2 = acc_ref[s, 1, pl.ds(LANES, 1), :]
            o = (acc_ref[s, 0, pl.ds(0, LANES), :] * (1.0 / l1)
                 - lam * (acc_ref[s, 1, pl.ds(0, LANES), :] * (1.0 / l2)))
            y = o * lax.rsqrt(jnp.mean(o * o, axis=0, keepdims=True) + EPS)
            i0 = pl.multiple_of(sb * kb + s * qb, qb)
            o_ref[pl.ds(i0, qb), :] = (y.T * subg_ref[...] * (1.0 - lam_init)).astype(o_ref.dtype)
        return carry

    assert per % 2 == 0
    lax.fori_loop(0, seq // kb, super_block, 0)


def _diff_attn(qkv, lq1, lk1, lq2, lk2, subg, *, lam_init):
    b, s, _ = qkv.shape
    qb, kb = DIFF_QB, DIFF_KB
    small = lambda w: pl.BlockSpec((1, w), lambda bi, h: (0, 0))
    return pl.pallas_call(
        functools.partial(_diff_kernel, seq=s, lam_init=lam_init),
        grid=(b, DIFF_HEADS),
        in_specs=[pl.BlockSpec(memory_space=pltpu.SMEM),
                  pl.BlockSpec((None, s, LANES), lambda bi, h: (bi, 0, h)),
                  pl.BlockSpec((None, s, LANES), lambda bi, h: (bi, 0, COLBLK_DK + h)),
                  pl.BlockSpec((None, s, LANES), lambda bi, h: (bi, 0, COLBLK_DV + h)),
                  small(DIFF_HD), small(DIFF_HD), small(DIFF_HD), small(DIFF_HD), small(2 * DIFF_HD)],
        out_specs=pl.BlockSpec((None, s, LANES), lambda bi, h: (bi, 0, h)),
        out_shape=jax.ShapeDtypeStruct((b, s, DIFF_W), BF16),
        scratch_shapes=[pltpu.VMEM((2, s // kb, kb, LANES), BF16),
                        pltpu.VMEM((s // kb, LANES + DIFF_LROWS, kb), BF16),
                        pltpu.VMEM((kb // qb, 2, LANES + DIFF_LROWS, qb), F32),
                        pltpu.VMEM((2, kb, qb), F32),
                        pltpu.VMEM((2, kb, qb), F32)],
        compiler_params=_cparams(("parallel", "parallel"), 56),
        name="diff_attn",
    )(_alibi_slopes(DIFF_HEADS), qkv, qkv, qkv,
      lq1.reshape(1, -1), lk1.reshape(1, -1), lq2.reshape(1, -1), lk2.reshape(1, -1), subg.reshape(1, -1))


def _dil_kernel(slopes_ref, *refs):
    ngrp = len(DIL_GROUPS)
    in_refs, (o_ref, qf_ref, kf_ref, vf_ref, og_ref, lg_ref) = refs[:5 * ngrp], refs[5 * ngrp:]
    ch, blk = DIL_CHUNK, DIL_BLK
    c = pl.program_id(1)
    slope = slopes_ref[pl.program_id(2)]
    scale = DIL_HD ** -0.5

    qi = lax.broadcasted_iota(jnp.int32, (blk, 2 * blk), 0)
    ki = lax.broadcasted_iota(jnp.int32, (blk, 2 * blk), 1)
    step = blk + qi - ki
    bias = jnp.where((step >= 0) & (step <= blk), -slope * step.astype(F32), NEG)
    seq_start = jnp.where(ki < blk, jnp.where(c == 0, NEG, 0.0), 0.0)

    for g, (window, dil) in enumerate(DIL_GROUPS):
        q_ref, k_ref, v_ref, kp_ref, vp_ref = in_refs[5 * g:5 * g + 5]
        span = blk * dil
        qf_ref[...] = q_ref[...].astype(F32)
        kf_ref[pl.ds(ch, ch), :] = k_ref[...].astype(F32)
        vf_ref[pl.ds(ch, ch), :] = v_ref[...].astype(F32)
        kf_ref[pl.ds(ch - span, span), :] = kp_ref[...].astype(F32)
        vf_ref[pl.ds(ch - span, span), :] = vp_ref[...].astype(F32)

        def unit(uidx, carry, g=g, dil=dil, span=span):
            n_local = uidx // dil
            r = uidx % dil
            q0 = n_local * span + r
            k0 = ch - span + q0
            q = qf_ref[pl.ds(q0, blk, stride=dil), :].astype(BF16)
            kk = kf_ref[pl.ds(k0, 2 * blk, stride=dil), :].astype(BF16)
            vv = vf_ref[pl.ds(k0, 2 * blk, stride=dil), :].astype(BF16)
            s = lax.dot_general(q, kk, (((1,), (1,)), ((), ())), preferred_element_type=F32) * scale
            s = s + bias
            s = s + jnp.where(n_local == 0, seq_start, jnp.zeros_like(seq_start))
            m = jnp.max(s, axis=-1, keepdims=True)
            p = jnp.exp(s - m)
            den = jnp.sum(p, axis=-1, keepdims=True)
            o = jnp.dot(p.astype(BF16), vv, preferred_element_type=F32) * (1.0 / den)
            lse = m + jnp.log(den)
            og_ref[g, pl.ds(q0, blk, stride=dil), :] = o
            lg_ref[g, pl.ds(q0, blk, stride=dil), :] = jnp.broadcast_to(lse, (blk, LANES))
            return carry

        lax.fori_loop(0, ch // blk, unit, 0, unroll=16)

    rows = 256
    for r0 in range(0, ch, rows):
        ls = [lg_ref[g, pl.ds(r0, rows), :] for g in range(ngrp)]
        top = functools.reduce(jnp.maximum, ls)
        ws = [jnp.exp(x - top) for x in ls]
        num = sum(w * og_ref[g, pl.ds(r0, rows), :] for g, w in enumerate(ws))
        o_ref[pl.ds(r0, rows), :] = (num * (1.0 / sum(ws))).astype(o_ref.dtype)


def _dil_attn(qkv):
    b, s, _ = qkv.shape
    ch, blk = DIL_CHUNK, DIL_BLK
    in_specs = [pl.BlockSpec(memory_space=pltpu.SMEM)]
    args = [_alibi_slopes(DIL_HEADS)]
    for g, (window, dil) in enumerate(DIL_GROUPS):
        span = blk * dil
        per = ch // span
        col = g * DIL_HEADS

        def cur(base, col=col):
            return pl.BlockSpec((None, ch, LANES), lambda bi, c, h: (bi, c, base + col + h))

        def prev(base, col=col, span=span, per=per):
            return pl.BlockSpec((None, span, LANES),
                                lambda bi, c, h: (bi, jnp.maximum(c * per - 1, 0), base + col + h))

        in_specs += [cur(COLBLK_SQ), cur(COLBLK_SK), cur(COLBLK_SV), prev(COLBLK_SK), prev(COLBLK_SV)]
        args += [qkv] * 5
    return pl.pallas_call(
        _dil_kernel,
        grid=(b, s // ch, DIL_HEADS),
        in_specs=in_specs,
        out_specs=pl.BlockSpec((None, ch, LANES), lambda bi, c, h: (bi, c, h)),
        out_shape=jax.ShapeDtypeStruct((b, s, DIL_OUT), BF16),
        scratch_shapes=[pltpu.VMEM((ch, LANES), F32),
                        pltpu.VMEM((2 * ch, LANES), F32),
                        pltpu.VMEM((2 * ch, LANES), F32),
                        pltpu.VMEM((len(DIL_GROUPS), ch, LANES), F32),
                        pltpu.VMEM((len(DIL_GROUPS), ch, LANES), F32)],
        compiler_params=_cparams(("parallel", "parallel", "parallel"), 48),
        name="dil_attn",
    )(*args)


def _mix_kernel(od_ref, ol_ref, gd_ref, gl_ref, x_ref, wbd_ref, wbl_ref, wo_ref, o_ref):
    yd = jnp.dot(od_ref[...], wbd_ref[...], preferred_element_type=F32)
    yl = jnp.dot(ol_ref[...], wbl_ref[...], preferred_element_type=F32)
    y = gd_ref[...] * yd + gl_ref[...] * yl
    o_ref[...] = x_ref[...] + jnp.dot(y.astype(BF16), wo_ref[...], preferred_element_type=F32)


def _mix_out(od, ol, gates, x2d, wbd, wbl, wo, *, rb):
    t, d = x2d.shape
    full = lambda a: pl.BlockSpec(a.shape, lambda i: (0, 0))
    return pl.pallas_call(
        _mix_kernel,
        grid=(t // rb,),
        in_specs=[pl.BlockSpec((rb, DIFF_W), lambda i: (i, 0)),
                  pl.BlockSpec((rb, DIL_OUT), lambda i: (i, 0)),
                  pl.BlockSpec((rb, d), lambda i: (i, 0)),
                  pl.BlockSpec((rb, d), lambda i: (i, 1)),
                  pl.BlockSpec((rb, d), lambda i: (i, 0)),
                  full(wbd), full(wbl), full(wo)],
        out_specs=pl.BlockSpec((rb, d), lambda i: (i, 0)),
        out_shape=jax.ShapeDtypeStruct((t, d), F32),
        compiler_params=_cparams(("parallel",), 48),
        name="mix_out",
    )(od, ol, gates, gates, x2d, wbd, wbl, wo)


def _memkv_kernel(mem_ref, g_ref, w_ref, kt_ref, v_ref):
    d = mem_ref.shape[-1]
    mn = _rms(mem_ref[...], g_ref[...]).astype(BF16)
    kv = jnp.dot(mn, w_ref[...], preferred_element_type=F32)
    kt_ref[...] = kv[:, :d].T.astype(BF16)
    v_ref[...] = kv[:, d:].astype(BF16)


def _mem_kv(mem, g, w):
    b, n, d = mem.shape
    return pl.pallas_call(
        _memkv_kernel,
        grid=(b,),
        in_specs=[pl.BlockSpec((None, n, d), lambda bi: (bi, 0, 0)),
                  pl.BlockSpec((1, d), lambda bi: (0, 0)),
                  pl.BlockSpec((d, 2 * d), lambda bi: (0, 0))],
        out_specs=[pl.BlockSpec((None, d, n), lambda bi: (bi, 0, 0)),
                   pl.BlockSpec((None, n, d), lambda bi: (bi, 0, 0))],
        out_shape=[jax.ShapeDtypeStruct((b, d, n), BF16), jax.ShapeDtypeStruct((b, n, d), BF16)],
        compiler_params=_cparams(("parallel",), 32),
        name="mem_kv",
    )(mem, g.reshape(1, d), w)


def _cross_kernel(x_ref, g_ref, wq_ref, kt_ref, v_ref, wo_ref, o_ref):
    d = x_ref.shape[-1]
    hd = d // X_HEADS
    x = x_ref[...]
    hq = _rms(x, g_ref[...]).astype(BF16)
    q = (jnp.dot(hq, wq_ref[...], preferred_element_type=F32) * (hd ** -0.5)).astype(BF16)
    outs = []
    for h in range(X_HEADS):
        sl = slice(h * hd, (h + 1) * hd)
        s = jnp.dot(q[:, sl], kt_ref[sl, :], preferred_element_type=F32)
        m = jnp.max(s, axis=-1, keepdims=True)
        p = jnp.exp(s - m)
        den = jnp.sum(p, axis=-1, keepdims=True)
        outs.append(jnp.dot(p.astype(BF16), v_ref[:, sl], preferred_element_type=F32) * (1.0 / den))
    o = jnp.concatenate(outs, axis=-1).astype(BF16)
    o_ref[...] = x + jnp.dot(o, wo_ref[...], preferred_element_type=F32)


def _cross_attn(x1, g, wq, kt, v, wo, *, rb):
    b, s, d = x1.shape
    n = v.shape[1]
    return pl.pallas_call(
        _cross_kernel,
        grid=(b, s // rb),
        in_specs=[pl.BlockSpec((None, rb, d), lambda bi, i: (bi, i, 0)),
                  pl.BlockSpec((1, d), lambda bi, i: (0, 0)),
                  pl.BlockSpec((d, d), lambda bi, i: (0, 0)),
                  pl.BlockSpec((None, d, n), lambda bi, i: (bi, 0, 0)),
                  pl.BlockSpec((None, n, d), lambda bi, i: (bi, 0, 0)),
                  pl.BlockSpec((d, d), lambda bi, i: (0, 0))],
        out_specs=pl.BlockSpec((None, rb, d), lambda bi, i: (bi, i, 0)),
        out_shape=jax.ShapeDtypeStruct((b, s, d), F32),
        compiler_params=_cparams(("parallel", "parallel"), 48),
        name="cross_attn",
    )(x1, g.reshape(1, d), wq, kt, v, wo)


def _ffn_kernel(x_ref, g_ref, wu_ref, cw_ref, cb_ref, wd_ref, fg_ref, o_ref, abuf_ref, carry_ref,
                *, blocks_per_seq, final_norm):
    rb = x_ref.shape[0]
    ff = wd_ref.shape[0]
    halo = CONV_HALO
    x = x_ref[...]
    h = _rms(x, g_ref[...]).astype(BF16)
    first = (pl.program_id(0) % blocks_per_seq) == 0
    acc = None
    for c0 in range(0, ff, FFN_CHUNK):
        fc = min(FFN_CHUNK, ff - c0)
        cols = pl.ds(c0, fc)
        a = jnp.dot(h, wu_ref[:, cols], preferred_element_type=F32)
        bgate = jnp.dot(h, wu_ref[:, pl.ds(ff + c0, fc)], preferred_element_type=F32)
        prev = carry_ref[:, cols]
        abuf_ref[pl.ds(0, halo), cols] = jnp.where(first, jnp.zeros_like(prev), prev)
        abuf_ref[pl.ds(halo, rb), cols] = a
        carry_ref[:, cols] = a[rb - halo:, :]
        cw = cw_ref[:, cols]
        y = (cw[2:3, :] * a + cw[1:2, :] * abuf_ref[pl.ds(halo - 1, rb), cols]
             + cw[0:1, :] * abuf_ref[pl.ds(halo - 2, rb), cols] + cb_ref[:, cols])
        act = 0.5 * y * (1.0 + lax.erf(y * (2.0 ** -0.5)))
        part = jnp.dot((act * bgate).astype(BF16), wd_ref[cols, :], preferred_element_type=F32)
        acc = part if acc is None else acc + part
    y = x + acc
    o_ref[...] = _rms(y, fg_ref[...]) if final_norm else y


def _ffn(x2d, g, w_up, cw, cb, w_down, fg, *, rb, blocks_per_seq, final_norm):
    t, d = x2d.shape
    ff = w_down.shape[0]
    assert ff % LANES == 0 and rb >= CONV_HALO
    return pl.pallas_call(
        functools.partial(_ffn_kernel, blocks_per_seq=blocks_per_seq, final_norm=final_norm),
        grid=(t // rb,),
        in_specs=[pl.BlockSpec((rb, d), lambda i: (i, 0)),
                  _resident((1, d)), _resident(w_up.shape), _resident((CONV_W, ff)), _resident((1, ff)),
                  _resident(w_down.shape), _resident((1, d))],
        out_specs=pl.BlockSpec((rb, d), lambda i: (i, 0)),
        out_shape=jax.ShapeDtypeStruct((t, d), F32),
        scratch_shapes=[pltpu.VMEM((rb + CONV_HALO, ff), F32),
                        pltpu.VMEM((CONV_HALO, ff), F32)],
        compiler_params=_cparams(("arbitrary",), 60),
        name="ffn",
    )(x2d, g.reshape(1, d), w_up, cw, cb.reshape(1, ff), w_down, fg.reshape(1, d))


def kernel(x, mem, mix_norm_g, w_in, b_gate, lam_q1, lam_k1, lam_q2, lam_k2, diff_subln_g,
           w_branch_diff, w_branch_dil, w_out, cross_norm_g, mem_norm_g, w_cq, w_ckv, w_co,
           ffn_norm_g, w_up, conv_w, conv_b, w_down, final_norm_g):
    b, s, d = x.shape
    depth = w_in.shape[0]
    t = b * s
    assert s % DIL_CHUNK == 0 and s % DIFF_KB == 0 and DIFF_KB % DIFF_QB == 0 and d == DIFF_W
    xf = x.reshape(t, d)
    for l in range(depth):
        lam_init = 0.8 - 0.6 * math.exp(-0.3 * l)
        qkv, gates = _inproj(xf, mix_norm_g[l], w_in[l].astype(BF16), b_gate[l], rb=512)
        qkv3 = qkv.reshape(b, s, QKV_W)
        o_diff = _diff_attn(qkv3, lam_q1[l], lam_k1[l], lam_q2[l], lam_k2[l], diff_subln_g[l],
                            lam_init=lam_init)
        o_dil = _dil_attn(qkv3)
        x1 = _mix_out(o_diff.reshape(t, DIFF_W), o_dil.reshape(t, DIL_OUT), gates, xf,
                      w_branch_diff[l].astype(BF16), w_branch_dil[l].astype(BF16),
                      w_out[l].astype(BF16), rb=512)
        mk_t, mv = _mem_kv(mem, mem_norm_g[l], w_ckv[l].astype(BF16))
        x2 = _cross_attn(x1.reshape(b, s, d), cross_norm_g[l], w_cq[l].astype(BF16), mk_t, mv,
                         w_co[l].astype(BF16), rb=512)
        xf = _ffn(x2.reshape(t, d), ffn_norm_g[l], w_up[l].astype(BF16), conv_w[l], conv_b[l],
                  w_down[l].astype(BF16), final_norm_g, rb=512, blocks_per_seq=s // 512,
                  final_norm=(l == depth - 1))
    return xf.reshape(b, s, d)
```

```python
import functools
import math

import jax
import jax.numpy as jnp
from jax import lax
from jax.experimental import pallas as pl
from jax.experimental.pallas import tpu as pltpu

F32 = jnp.float32
BF16 = jnp.bfloat16

EPS = 1e-5
NEG = -1e30
LOG2E = math.log2(math.e)
LANES = 128

DIFF_HEADS = 8
DIFF_HD = 64
DIFF_W = DIFF_HEADS * 2 * DIFF_HD
DIFF_QB = 256
DIFF_KB = 1024
DIFF_LROWS = 16
DIL_GROUPS = ((128, 1), (512, 4), (2048, 16))
DIL_HEADS = 4
DIL_HD = 128
DIL_W = len(DIL_GROUPS) * DIL_HEADS * DIL_HD
DIL_OUT = DIL_HEADS * DIL_HD
DIL_BLK = 128
DIL_CHUNK = 2048
DIL_SUBSTRIDE = 4
X_HEADS = 4
CONV_W = 3
CONV_HALO = 8
FFN_CHUNK = 1024

QKV_W = 3 * DIFF_W + 3 * DIL_W
INPROJ_NC = 512
COLBLK_DK = DIFF_W // LANES
COLBLK_DV = 2 * DIFF_W // LANES
COLBLK_SQ = 3 * DIFF_W // LANES
COLBLK_SK = COLBLK_SQ + DIL_W // LANES
COLBLK_SV = COLBLK_SK + DIL_W // LANES


def _alibi_slopes(n):
    return jnp.exp2(-8.0 * jnp.arange(1, n + 1, dtype=F32) / n)


def _cparams(sem, vmem_mb):
    return pltpu.CompilerParams(dimension_semantics=sem, vmem_limit_bytes=vmem_mb * 1024 * 1024)


def _rms(x, g):
    return x * lax.rsqrt(jnp.mean(x * x, axis=-1, keepdims=True) + EPS) * g


def _resident(shape):
    return pl.BlockSpec(shape, lambda *_: (0,) * len(shape), pipeline_mode=pl.Buffered(1))


def _inproj_kernel(x_ref, g_ref, w_ref, b_ref, qkv_ref, gate_ref):
    h = _rms(x_ref[...], g_ref[...]).astype(BF16)
    nq, ng, nc = qkv_ref.shape[1], gate_ref.shape[1], INPROJ_NC
    for c0 in range(0, nq, nc):
        acc = jnp.dot(h, w_ref[:, pl.ds(c0, nc)], preferred_element_type=F32)
        if c0 < DIFF_W:
            acc = acc * (LOG2E * DIFF_HD ** -0.5)
        elif 3 * DIFF_W <= c0 < 3 * DIFF_W + DIL_W:
            acc = acc * (LOG2E * DIL_HD ** -0.5)
        qkv_ref[:, pl.ds(c0, nc)] = acc.astype(qkv_ref.dtype)
    for c0 in range(0, ng, nc):
        acc = jnp.dot(h, w_ref[:, pl.ds(nq + c0, nc)], preferred_element_type=F32)
        gate_ref[:, pl.ds(c0, nc)] = jax.nn.sigmoid(acc + b_ref[:, pl.ds(c0, nc)])


def _inproj(x2d, g, w, b_gate, *, rb):
    t, d = x2d.shape
    ng = b_gate.shape[0]
    nq = w.shape[1] - ng
    assert DIFF_W % INPROJ_NC == 0 and DIL_W % INPROJ_NC == 0 and ng % INPROJ_NC == 0 and nq == QKV_W
    return pl.pallas_call(
        _inproj_kernel,
        grid=(t // rb,),
        in_specs=[pl.BlockSpec((rb, d), lambda i: (i, 0)),
                  _resident((1, d)), _resident(w.shape), _resident((1, ng))],
        out_specs=[pl.BlockSpec((rb, nq), lambda i: (i, 0)),
                   pl.BlockSpec((rb, ng), lambda i: (i, 0))],
        out_shape=[jax.ShapeDtypeStruct((t, nq), BF16), jax.ShapeDtypeStruct((t, ng), F32)],
        compiler_params=_cparams(("parallel",), 60),
        name="inproj",
    )(x2d, g.reshape(1, d), w, b_gate.reshape(1, ng))


def _diff_kernel(slopes_ref, q_ref, k_ref, v_ref, lq1_ref, lk1_ref, lq2_ref, lk2_ref, subg_ref,
                 o_ref, kaug_ref, vt_ref, acc_ref, ta_ref, tb_ref, *, seq, lam_init):
    qb, kb = DIFF_QB, DIFF_KB
    per = kb // qb
    slope = slopes_ref[pl.program_id(1)] * LOG2E

    lane_k = lax.broadcasted_iota(jnp.int32, (kb, LANES), 1)
    val = slope * lax.broadcasted_iota(jnp.int32, (kb, LANES), 0).astype(F32)
    hi = val.astype(BF16).astype(F32)
    mid = (val - hi).astype(BF16).astype(F32)
    lo = val - hi - mid
    zero_k = jnp.zeros((kb, LANES), F32)

    def bias_cols(first):
        return jnp.where(lane_k == first, hi,
                         jnp.where(lane_k == first + 1, mid, jnp.where(lane_k == first + 2, lo, zero_k)))

    bias1, bias2 = bias_cols(DIFF_HD), bias_cols(0)
    ones_row = jnp.where(lax.broadcasted_iota(jnp.int32, (DIFF_LROWS, kb), 0) == 0, 1.0, 0.0).astype(BF16)

    def build(jb, carry):
        r0 = pl.multiple_of(jb * kb, kb)
        kk = k_ref[pl.ds(r0, kb), :].astype(F32)
        k1 = jnp.where(lane_k < DIFF_HD, kk, bias1)
        k2 = jnp.where(lane_k >= DIFF_HD, kk, bias2)
        kaug_ref[0, jb] = k1.astype(BF16)
        kaug_ref[1, jb] = k2.astype(BF16)
        vt_ref[jb, pl.ds(0, LANES), :] = v_ref[pl.ds(r0, kb), :].astype(F32).T.astype(BF16)
        vt_ref[jb, pl.ds(LANES, DIFF_LROWS), :] = ones_row
        return carry

    lax.fori_loop(0, seq // kb, build, 0)

    lam = (jnp.exp(jnp.sum(lq1_ref[...] * lk1_ref[...], axis=-1, keepdims=True))
           - jnp.exp(jnp.sum(lq2_ref[...] * lk2_ref[...], axis=-1, keepdims=True)) + lam_init)
    lane_q = lax.broadcasted_iota(jnp.int32, (qb, LANES), 1)
    one_q = jnp.ones((qb, LANES), F32)
    zero_q = jnp.zeros((qb, LANES), F32)
    qpos = lax.broadcasted_iota(jnp.int32, (1, qb), 1).astype(F32)
    causal = (lax.broadcasted_iota(jnp.int32, (qb, qb), 0)
              <= lax.broadcasted_iota(jnp.int32, (qb, qb), 1))
    bufs = (ta_ref, tb_ref)

    def scores(q_t, jb, nk, t_ref, diag):
        mx = []
        for c in range(2):
            t = jnp.dot(kaug_ref[c, jb, pl.ds(0, nk), :], q_t[c], preferred_element_type=F32)
            if diag:
                tri = jnp.where(causal, t[nk - qb:, :], NEG)
                t_ref[c, pl.ds(nk - qb, qb), :] = tri
                m = jnp.max(tri, axis=0, keepdims=True)
                if nk > qb:
                    t_ref[c, pl.ds(0, nk - qb), :] = t[:nk - qb, :]
                    m = jnp.maximum(m, jnp.max(t[:nk - qb, :], axis=0, keepdims=True))
            else:
                t_ref[c] = t
                m = jnp.max(t, axis=0, keepdims=True)
            mx.append(m)
        return tuple(mx)

    def accumulate(strip, jb, nk, dist0, t_ref, mx, state):
        u = slope * (jnp.asarray(dist0, F32) + qpos)
        new_state = []
        for c in range(2):
            m_new = mx[c] - u if state is None else jnp.maximum(state[c], mx[c] - u)
            p = jnp.exp2(t_ref[c, pl.ds(0, nk), :] - (m_new + u))
            pv = jnp.dot(vt_ref[jb, :, pl.ds(0, nk)], p.astype(BF16), preferred_element_type=F32)
            if state is None:
                acc_ref[strip, c] = pv
            else:
                acc_ref[strip, c] = jnp.exp2(state[c] - m_new) * acc_ref[strip, c] + pv
            new_state.append(m_new)
        return tuple(new_state)

    def super_block(sb, carry):
        q_t = []
        for s in range(per):
            i0 = pl.multiple_of(sb * kb + s * qb, qb)
            qq = q_ref[pl.ds(i0, qb), :].astype(F32)
            q1 = jnp.where(lane_q < DIFF_HD, qq, jnp.where(lane_q < DIFF_HD + 3, one_q, zero_q))
            q2 = jnp.where(lane_q >= DIFF_HD, qq, jnp.where(lane_q < 3, one_q, zero_q))
            q_t.append((q1.T.astype(BF16), q2.T.astype(BF16)))

        state = []
        mx = scores(q_t[0], sb, qb, bufs[0], True)
        for s in range(per):
            if s + 1 < per:
                mx_next = scores(q_t[s + 1], sb, (s + 2) * qb, bufs[(s + 1) % 2], True)
            else:
                mx_next = scores(q_t[0], jnp.maximum(sb - 1, 0), kb, bufs[per % 2], False)
            state.append(accumulate(s, sb, (s + 1) * qb, s * qb, bufs[s % 2], mx, None))
            mx = mx_next

        def full_tile(n, carry):
            mx, state = carry
            jb = sb - 1 - n
            dist = (n + 1) * kb
            state = list(state)
            for s in range(per):
                if s + 1 < per:
                    mx_next = scores(q_t[s + 1], jb, kb, bufs[(per + s + 1) % 2], False)
                else:
                    mx_next = scores(q_t[0], jnp.maximum(jb - 1, 0), kb, bufs[per % 2], False)
                state[s] = accumulate(s, jb, kb, dist + s * qb, bufs[(per + s) % 2], mx, state[s])
                mx = mx_next
            return mx, tuple(state)

        _, state = lax.fori_loop(0, sb, full_tile, (mx, tuple(state)))

        for s in range(per):
            l1 = acc_ref[s, 0, pl.ds(LANES, 1), :]
            l2 = acc_ref[s, 1, pl.ds(LANES, 1), :]
            o = (acc_ref[s, 0, pl.ds(0, LANES), :] * (1.0 / l1)
                 - lam * (acc_ref[s, 1, pl.ds(0, LANES), :] * (1.0 / l2)))
            y = o * lax.rsqrt(jnp.mean(o * o, axis=0, keepdims=True) + EPS)
            i0 = pl.multiple_of(sb * kb + s * qb, qb)
            o_ref[pl.ds(i0, qb), :] = (y.T * subg_ref[...] * (1.0 - lam_init)).astype(o_ref.dtype)
        return carry

    assert per % 2 == 0
    lax.fori_loop(0, seq // kb, super_block, 0)


def _diff_attn(qkv, lq1, lk1, lq2, lk2, subg, *, lam_init):
    b, s, _ = qkv.shape
    qb, kb = DIFF_QB, DIFF_KB
    small = lambda w: pl.BlockSpec((1, w), lambda bi, h: (0, 0))
    return pl.pallas_call(
        functools.partial(_diff_kernel, seq=s, lam_init=lam_init),
        grid=(b, DIFF_HEADS),
        in_specs=[pl.BlockSpec(memory_space=pltpu.SMEM),
                  pl.BlockSpec((None, s, LANES), lambda bi, h: (bi, 0, h)),
                  pl.BlockSpec((None, s, LANES), lambda bi, h: (bi, 0, COLBLK_DK + h)),
                  pl.BlockSpec((None, s, LANES), lambda bi, h: (bi, 0, COLBLK_DV + h)),
                  small(DIFF_HD), small(DIFF_HD), small(DIFF_HD), small(DIFF_HD), small(2 * DIFF_HD)],
        out_specs=pl.BlockSpec((None, s, LANES), lambda bi, h: (bi, 0, h)),
        out_shape=jax.ShapeDtypeStruct((b, s, DIFF_W), BF16),
        scratch_shapes=[pltpu.VMEM((2, s // kb, kb, LANES), BF16),
                        pltpu.VMEM((s // kb, LANES + DIFF_LROWS, kb), BF16),
                        pltpu.VMEM((kb // qb, 2, LANES + DIFF_LROWS, qb), F32),
                        pltpu.VMEM((2, kb, qb), F32),
                        pltpu.VMEM((2, kb, qb), F32)],
        compiler_params=_cparams(("parallel", "parallel"), 56),
        name="diff_attn",
    )(_alibi_slopes(DIFF_HEADS), qkv, qkv, qkv,
      lq1.reshape(1, -1), lk1.reshape(1, -1), lq2.reshape(1, -1), lk2.reshape(1, -1), subg.reshape(1, -1))


def _dil_kernel(slopes_ref, *refs):
    ngrp = len(DIL_GROUPS)
    in_refs = refs[:5 * ngrp]
    o_ref, qf_ref, kf_ref, vf_ref, qy_ref, ky_ref, vy_ref, og_ref, lg_ref = refs[5 * ngrp:]
    ch, blk, sub = DIL_CHUNK, DIL_BLK, DIL_SUBSTRIDE
    c = pl.program_id(1)
    slope = slopes_ref[pl.program_id(2)] * LOG2E

    qi = lax.broadcasted_iota(jnp.int32, (blk, 2 * blk), 0)
    ki = lax.broadcasted_iota(jnp.int32, (blk, 2 * blk), 1)
    step = blk + qi - ki
    bias = jnp.where((step >= 0) & (step <= blk), -slope * step.astype(F32), NEG)
    seq_start = jnp.where(ki < blk, jnp.where(c == 0, NEG, 0.0), 0.0)

    def attend(q, kk, vv, first_block):
        s = lax.dot_general(q, kk, (((1,), (1,)), ((), ())), preferred_element_type=F32) + bias
        if first_block:
            s = s + seq_start
        m = jnp.max(s, axis=-1, keepdims=True)
        p = jnp.exp2(s - m)
        den = jnp.sum(p, axis=-1, keepdims=True)
        o = jnp.dot(p.astype(BF16), vv, preferred_element_type=F32) * (1.0 / den)
        return o, jnp.broadcast_to(m + jnp.log2(den), (blk, LANES))

    def in_batches(unit):
        for u in range(ch // blk):
            unit(u)

    for g, (window, dil) in enumerate(DIL_GROUPS):
        q_ref, k_ref, v_ref, kp_ref, vp_ref = in_refs[5 * g:5 * g + 5]
        span = blk * dil
        if dil == 1:
            def unit(n, g=g, q_ref=q_ref, k_ref=k_ref, v_ref=v_ref, kp_ref=kp_ref, vp_ref=vp_ref):
                q = q_ref[pl.ds(n * blk, blk), :]
                if n == 0:
                    kk = jnp.concatenate([kp_ref[...], k_ref[pl.ds(0, blk), :]], axis=0)
                    vv = jnp.concatenate([vp_ref[...], v_ref[pl.ds(0, blk), :]], axis=0)
                else:
                    kk = k_ref[pl.ds((n - 1) * blk, 2 * blk), :]
                    vv = v_ref[pl.ds((n - 1) * blk, 2 * blk), :]
                o, lse = attend(q, kk, vv, n == 0)
                og_ref[g, pl.ds(n * blk, blk), :] = o
                lg_ref[g, pl.ds(n * blk, blk), :] = lse

            in_batches(unit)
            continue

        qf_ref[...] = q_ref[...].astype(F32)
        kf_ref[pl.ds(ch, ch), :] = k_ref[...].astype(F32)
        vf_ref[pl.ds(ch, ch), :] = v_ref[...].astype(F32)
        kf_ref[pl.ds(ch - span, span), :] = kp_ref[...].astype(F32)
        vf_ref[pl.ds(ch - span, span), :] = vp_ref[...].astype(F32)
        two_stage = dil > sub
        if two_stage:
            assert span == ch and dil % sub == 0
            for r0 in range(sub):
                qy_ref[r0] = qf_ref[pl.ds(r0, ch // sub, stride=sub), :]
                ky_ref[r0] = kf_ref[pl.ds(r0, 2 * ch // sub, stride=sub), :]
                vy_ref[r0] = vf_ref[pl.ds(r0, 2 * ch // sub, stride=sub), :]

        def unit(uidx, g=g, dil=dil, span=span, two_stage=two_stage):
            n_local, r = divmod(uidx, dil)
            q0 = n_local * span + r
            k0 = ch - span + q0
            if two_stage:
                r1, r0 = divmod(r, sub)
                q = qy_ref[r0, pl.ds(r1, blk, stride=dil // sub), :]
                kk = ky_ref[r0, pl.ds(r1, 2 * blk, stride=dil // sub), :]
                vv = vy_ref[r0, pl.ds(r1, 2 * blk, stride=dil // sub), :]
            else:
                q = qf_ref[pl.ds(q0, blk, stride=dil), :]
                kk = kf_ref[pl.ds(k0, 2 * blk, stride=dil), :]
                vv = vf_ref[pl.ds(k0, 2 * blk, stride=dil), :]
            o, lse = attend(q.astype(BF16), kk.astype(BF16), vv.astype(BF16), n_local == 0)
            og_ref[g, pl.ds(q0, blk, stride=dil), :] = o
            lg_ref[g, pl.ds(q0, blk, stride=dil), :] = lse

        in_batches(unit)

    rows = 256
    for r0 in range(0, ch, rows):
        ls = [lg_ref[g, pl.ds(r0, rows), :] for g in range(ngrp)]
        top = functools.reduce(jnp.maximum, ls)
        ws = [jnp.exp2(x - top) for x in ls]
        num = sum(w * og_ref[g, pl.ds(r0, rows), :] for g, w in enumerate(ws))
        o_ref[pl.ds(r0, rows), :] = (num * (1.0 / sum(ws))).astype(o_ref.dtype)


def _dil_attn(qkv):
    b, s, _ = qkv.shape
    ch, blk = DIL_CHUNK, DIL_BLK
    in_specs = [pl.BlockSpec(memory_space=pltpu.SMEM)]
    args = [_alibi_slopes(DIL_HEADS)]
    for g, (window, dil) in enumerate(DIL_GROUPS):
        span = blk * dil
        per = ch // span
        col = g * DIL_HEADS

        def cur(base, col=col):
            return pl.BlockSpec((None, ch, LANES), lambda bi, c, h: (bi, c, base + col + h))

        def prev(base, col=col, span=span, per=per):
            return pl.BlockSpec((None, span, LANES),
                                lambda bi, c, h: (bi, jnp.maximum(c * per - 1, 0), base + col + h))

        in_specs += [cur(COLBLK_SQ), cur(COLBLK_SK), cur(COLBLK_SV), prev(COLBLK_SK), prev(COLBLK_SV)]
        args += [qkv] * 5
    return pl.pallas_call(
        _dil_kernel,
        grid=(b, s // ch, DIL_HEADS),
        in_specs=in_specs,
        out_specs=pl.BlockSpec((None, ch, LANES), lambda bi, c, h: (bi, c, h)),
        out_shape=jax.ShapeDtypeStruct((b, s, DIL_OUT), BF16),
        scratch_shapes=[pltpu.VMEM((ch, LANES), F32),
                        pltpu.VMEM((2 * ch, LANES), F32),
                        pltpu.VMEM((2 * ch, LANES), F32),
                        pltpu.VMEM((DIL_SUBSTRIDE, ch // DIL_SUBSTRIDE, LANES), F32),
                        pltpu.VMEM((DIL_SUBSTRIDE, 2 * ch // DIL_SUBSTRIDE, LANES), F32),
                        pltpu.VMEM((DIL_SUBSTRIDE, 2 * ch // DIL_SUBSTRIDE, LANES), F32),
                        pltpu.VMEM((len(DIL_GROUPS), ch, LANES), F32),
                        pltpu.VMEM((len(DIL_GROUPS), ch, LANES), F32)],
        compiler_params=_cparams(("parallel", "parallel", "parallel"), 48),
        name="dil_attn",
    )(*args)


def _mix_kernel(od_ref, ol_ref, gd_ref, gl_ref, x_ref, wbd_ref, wbl_ref, wo_ref, o_ref):
    yd = jnp.dot(od_ref[...], wbd_ref[...], preferred_element_type=F32)
    yl = jnp.dot(ol_ref[...], wbl_ref[...], preferred_element_type=F32)
    y = gd_ref[...] * yd + gl_ref[...] * yl
    o_ref[...] = x_ref[...] + jnp.dot(y.astype(BF16), wo_ref[...], preferred_element_type=F32)


def _mix_out(od, ol, gates, x2d, wbd, wbl, wo, *, rb):
    t, d = x2d.shape
    return pl.pallas_call(
        _mix_kernel,
        grid=(t // rb,),
        in_specs=[pl.BlockSpec((rb, DIFF_W), lambda i: (i, 0)),
                  pl.BlockSpec((rb, DIL_OUT), lambda i: (i, 0)),
                  pl.BlockSpec((rb, d), lambda i: (i, 0)),
                  pl.BlockSpec((rb, d), lambda i: (i, 1)),
                  pl.BlockSpec((rb, d), lambda i: (i, 0)),
                  _resident(wbd.shape), _resident(wbl.shape), _resident(wo.shape)],
        out_specs=pl.BlockSpec((rb, d), lambda i: (i, 0)),
        out_shape=jax.ShapeDtypeStruct((t, d), F32),
        compiler_params=_cparams(("parallel",), 56),
        name="mix_out",
    )(od, ol, gates, gates, x2d, wbd, wbl, wo)


def _memkv_kernel(mem_ref, g_ref, w_ref, kt_ref, v_ref):
    d = mem_ref.shape[-1]
    mn = _rms(mem_ref[...], g_ref[...]).astype(BF16)
    kv = jnp.dot(mn, w_ref[...], preferred_element_type=F32)
    kt_ref[...] = kv[:, :d].T.astype(BF16)
    v_ref[...] = kv[:, d:].astype(BF16)


def _mem_kv(mem, g, w):
    b, n, d = mem.shape
    return pl.pallas_call(
        _memkv_kernel,
        grid=(b,),
        in_specs=[pl.BlockSpec((None, n, d), lambda bi: (bi, 0, 0)),
                  pl.BlockSpec((1, d), lambda bi: (0, 0)),
                  pl.BlockSpec((d, 2 * d), lambda bi: (0, 0))],
        out_specs=[pl.BlockSpec((None, d, n), lambda bi: (bi, 0, 0)),
                   pl.BlockSpec((None, n, d), lambda bi: (bi, 0, 0))],
        out_shape=[jax.ShapeDtypeStruct((b, d, n), BF16), jax.ShapeDtypeStruct((b, n, d), BF16)],
        compiler_params=_cparams(("parallel",), 32),
        name="mem_kv",
    )(mem, g.reshape(1, d), w)


def _cross_kernel(x_ref, g_ref, wq_ref, kt_ref, v_ref, wo_ref, o_ref):
    d = x_ref.shape[-1]
    hd = d // X_HEADS
    x = x_ref[...]
    hq = _rms(x, g_ref[...]).astype(BF16)
    q = (jnp.dot(hq, wq_ref[...], preferred_element_type=F32) * (hd ** -0.5)).astype(BF16)
    outs = []
    for h in range(X_HEADS):
        sl = slice(h * hd, (h + 1) * hd)
        s = jnp.dot(q[:, sl], kt_ref[sl, :], preferred_element_type=F32)
        m = jnp.max(s, axis=-1, keepdims=True)
        p = jnp.exp(s - m)
        den = jnp.sum(p, axis=-1, keepdims=True)
        outs.append(jnp.dot(p.astype(BF16), v_ref[:, sl], preferred_element_type=F32) * (1.0 / den))
    o = jnp.concatenate(outs, axis=-1).astype(BF16)
    o_ref[...] = x + jnp.dot(o, wo_ref[...], preferred_element_type=F32)


def _cross_attn(x1, g, wq, kt, v, wo, *, rb):
    b, s, d = x1.shape
    n = v.shape[1]
    return pl.pallas_call(
        _cross_kernel,
        grid=(b, s // rb),
        in_specs=[pl.BlockSpec((None, rb, d), lambda bi, i: (bi, i, 0)),
                  _resident((1, d)), _resident((d, d)),
                  pl.BlockSpec((None, d, n), lambda bi, i: (bi, 0, 0)),
                  pl.BlockSpec((None, n, d), lambda bi, i: (bi, 0, 0)),
                  _resident((d, d))],
        out_specs=pl.BlockSpec((None, rb, d), lambda bi, i: (bi, i, 0)),
        out_shape=jax.ShapeDtypeStruct((b, s, d), F32),
        compiler_params=_cparams(("parallel", "parallel"), 48),
        name="cross_attn",
    )(x1, g.reshape(1, d), wq, kt, v, wo)


def _ffn_kernel(x_ref, g_ref, wu_ref, cw_ref, cb_ref, wd_ref, fg_ref, o_ref, abuf_ref, carry_ref,
                *, blocks_per_seq, final_norm):
    rb = x_ref.shape[0]
    ff = wd_ref.shape[0]
    halo = CONV_HALO
    x = x_ref[...]
    h = _rms(x, g_ref[...]).astype(BF16)
    first = (pl.program_id(0) % blocks_per_seq) == 0
    acc = None
    for c0 in range(0, ff, FFN_CHUNK):
        fc = min(FFN_CHUNK, ff - c0)
        cols = pl.ds(c0, fc)
        a = jnp.dot(h, wu_ref[:, cols], preferred_element_type=F32)
        bgate = jnp.dot(h, wu_ref[:, pl.ds(ff + c0, fc)], preferred_element_type=F32)
        prev = carry_ref[:, cols]
        abuf_ref[pl.ds(0, halo), cols] = jnp.where(first, jnp.zeros_like(prev), prev)
        abuf_ref[pl.ds(halo, rb), cols] = a
        carry_ref[:, cols] = a[rb - halo:, :]
        cw = cw_ref[:, cols]
        y = (cw[2:3, :] * a + cw[1:2, :] * abuf_ref[pl.ds(halo - 1, rb), cols]
             + cw[0:1, :] * abuf_ref[pl.ds(halo - 2, rb), cols] + cb_ref[:, cols])
        act = 0.5 * y * (1.0 + lax.erf(y * (2.0 ** -0.5)))
        part = jnp.dot((act * bgate).astype(BF16), wd_ref[cols, :], preferred_element_type=F32)
        acc = part if acc is None else acc + part
    y = x + acc
    o_ref[...] = _rms(y, fg_ref[...]) if final_norm else y


def _ffn(x2d, g, w_up, cw, cb, w_down, fg, *, rb, blocks_per_seq, final_norm):
    t, d = x2d.shape
    ff = w_down.shape[0]
    assert ff % LANES == 0 and rb >= CONV_HALO
    return pl.pallas_call(
        functools.partial(_ffn_kernel, blocks_per_seq=blocks_per_seq, final_norm=final_norm),
        grid=(t // rb,),
        in_specs=[pl.BlockSpec((rb, d), lambda i: (i, 0)),
                  _resident((1, d)), _resident(w_up.shape), _resident((CONV_W, ff)), _resident((1, ff)),
                  _resident(w_down.shape), _resident((1, d))],
        out_specs=pl.BlockSpec((rb, d), lambda i: (i, 0)),
        out_shape=jax.ShapeDtypeStruct((t, d), F32),
        scratch_shapes=[pltpu.VMEM((rb + CONV_HALO, ff), F32),
                        pltpu.VMEM((CONV_HALO, ff), F32)],
        compiler_params=_cparams(("arbitrary",), 60),
        name="ffn",
    )(x2d, g.reshape(1, d), w_up, cw, cb.reshape(1, ff), w_down, fg.reshape(1, d))


def kernel(x, mem, mix_norm_g, w_in, b_gate, lam_q1, lam_k1, lam_q2, lam_k2, diff_subln_g,
           w_branch_diff, w_branch_dil, w_out, cross_norm_g, mem_norm_g, w_cq, w_ckv, w_co,
           ffn_norm_g, w_up, conv_w, conv_b, w_down, final_norm_g):
    b, s, d = x.shape
    depth = w_in.shape[0]
    t = b * s
    assert s % DIL_CHUNK == 0 and s % DIFF_KB == 0 and DIFF_KB % DIFF_QB == 0 and d == DIFF_W
    xf = x.reshape(t, d)
    for l in range(depth):
        lam_init = 0.8 - 0.6 * math.exp(-0.3 * l)
        qkv, gates = _inproj(xf, mix_norm_g[l], w_in[l].astype(BF16), b_gate[l], rb=512)
        qkv3 = qkv.reshape(b, s, QKV_W)
        o_diff = _diff_attn(qkv3, lam_q1[l], lam_k1[l], lam_q2[l], lam_k2[l], diff_subln_g[l],
                            lam_init=lam_init)
        o_dil = _dil_attn(qkv3)
        x1 = _mix_out(o_diff.reshape(t, DIFF_W), o_dil.reshape(t, DIL_OUT), gates, xf,
                      w_branch_diff[l].astype(BF16), w_branch_dil[l].astype(BF16),
                      w_out[l].astype(BF16), rb=1024)
        mk_t, mv = _mem_kv(mem, mem_norm_g[l], w_ckv[l].astype(BF16))
        x2 = _cross_attn(x1.reshape(b, s, d), cross_norm_g[l], w_cq[l].astype(BF16), mk_t, mv,
                         w_co[l].astype(BF16), rb=1024)
        xf = _ffn(x2.reshape(t, d), ffn_norm_g[l], w_up[l].astype(BF16), conv_w[l], conv_b[l],
                  w_down[l].astype(BF16), final_norm_g, rb=512, blocks_per_seq=s // 512,
                  final_norm=(l == depth - 1))
    return xf.reshape(b, s, d)
```

```python
import functools
import math

import jax
import jax.numpy as jnp
from jax import lax
from jax.experimental import pallas as pl
from jax.experimental.pallas import tpu as pltpu

F32 = jnp.float32
BF16 = jnp.bfloat16

EPS = 1e-5
NEG = -1e30
LOG2E = math.log2(math.e)
LANES = 128

DIFF_HEADS = 8
DIFF_HD = 64
DIFF_W = DIFF_HEADS * 2 * DIFF_HD
DIFF_QB = 256
DIFF_KB = 1024
DIFF_LROWS = 16
DIL_GROUPS = ((128, 1), (512, 4), (2048, 16))
DIL_HEADS = 4
DIL_HD = 128
DIL_W = len(DIL_GROUPS) * DIL_HEADS * DIL_HD
DIL_OUT = DIL_HEADS * DIL_HD
DIL_BLK = 128
DIL_CHUNK = 2048
DIL_SUBSTRIDE = 4
X_HEADS = 4
CONV_W = 3
CONV_HALO = 8
FFN_CHUNK = 1024

INPROJ_RB, INPROJ_VMEM = 512, 60
DIFF_VMEM = 56
DIL_VMEM = 48
MIX_RB, MIX_VMEM = 1024, 56
MEMKV_VMEM = 32
CROSS_RB, CROSS_VMEM = 1024, 48
FFN_RB, FFN_VMEM = 512, 60

QKV_W = 3 * DIFF_W + 3 * DIL_W
INPROJ_NC = 512
COLBLK_DK = DIFF_W // LANES
COLBLK_DV = 2 * DIFF_W // LANES
COLBLK_SQ = 3 * DIFF_W // LANES
COLBLK_SK = COLBLK_SQ + DIL_W // LANES
COLBLK_SV = COLBLK_SK + DIL_W // LANES


def _alibi_slopes(n):
    return jnp.exp2(-8.0 * jnp.arange(1, n + 1, dtype=F32) / n)


def _cparams(sem, vmem_mb):
    return pltpu.CompilerParams(dimension_semantics=sem, vmem_limit_bytes=vmem_mb * 1024 * 1024)


def _rms(x, g):
    return x * lax.rsqrt(jnp.mean(x * x, axis=-1, keepdims=True) + EPS) * g


def _resident(shape):
    return pl.BlockSpec(shape, lambda *_: (0,) * len(shape), pipeline_mode=pl.Buffered(1))


def _inproj_kernel(x_ref, g_ref, w_ref, b_ref, qkv_ref, gate_ref):
    h = _rms(x_ref[...], g_ref[...]).astype(BF16)
    nq, ng, nc = qkv_ref.shape[1], gate_ref.shape[1], INPROJ_NC
    for c0 in range(0, nq, nc):
        acc = jnp.dot(h, w_ref[:, pl.ds(c0, nc)], preferred_element_type=F32)
        if c0 < DIFF_W:
            acc = acc * (LOG2E * DIFF_HD ** -0.5)
        elif 3 * DIFF_W <= c0 < 3 * DIFF_W + DIL_W:
            acc = acc * (LOG2E * DIL_HD ** -0.5)
        qkv_ref[:, pl.ds(c0, nc)] = acc.astype(qkv_ref.dtype)
    for c0 in range(0, ng, nc):
        acc = jnp.dot(h, w_ref[:, pl.ds(nq + c0, nc)], preferred_element_type=F32)
        gate_ref[:, pl.ds(c0, nc)] = jax.nn.sigmoid(acc + b_ref[:, pl.ds(c0, nc)])


def _inproj(x2d, g, w, b_gate):
    t, d = x2d.shape
    rb = INPROJ_RB
    ng = b_gate.shape[0]
    nq = w.shape[1] - ng
    assert DIFF_W % INPROJ_NC == 0 and DIL_W % INPROJ_NC == 0 and ng % INPROJ_NC == 0 and nq == QKV_W
    return pl.pallas_call(
        _inproj_kernel,
        grid=(t // rb,),
        in_specs=[pl.BlockSpec((rb, d), lambda i: (i, 0)),
                  _resident((1, d)), _resident(w.shape), _resident((1, ng))],
        out_specs=[pl.BlockSpec((rb, nq), lambda i: (i, 0)),
                   pl.BlockSpec((rb, ng), lambda i: (i, 0))],
        out_shape=[jax.ShapeDtypeStruct((t, nq), BF16), jax.ShapeDtypeStruct((t, ng), F32)],
        compiler_params=_cparams(("parallel",), INPROJ_VMEM),
        name="inproj",
    )(x2d, g.reshape(1, d), w, b_gate.reshape(1, ng))


def _diff_kernel(slopes_ref, q_ref, k_ref, v_ref, lq1_ref, lk1_ref, lq2_ref, lk2_ref, subg_ref,
                 o_ref, kaug_ref, vt_ref, acc_ref, ta_ref, tb_ref, *, seq, lam_init):
    qb, kb = DIFF_QB, DIFF_KB
    per = kb // qb
    slope = slopes_ref[pl.program_id(1)] * LOG2E

    lane_k = lax.broadcasted_iota(jnp.int32, (kb, LANES), 1)
    val = slope * lax.broadcasted_iota(jnp.int32, (kb, LANES), 0).astype(F32)
    hi = val.astype(BF16).astype(F32)
    mid = (val - hi).astype(BF16).astype(F32)
    lo = val - hi - mid
    zero_k = jnp.zeros((kb, LANES), F32)

    def bias_cols(first):
        return jnp.where(lane_k == first, hi,
                         jnp.where(lane_k == first + 1, mid, jnp.where(lane_k == first + 2, lo, zero_k)))

    bias1, bias2 = bias_cols(DIFF_HD), bias_cols(0)
    ones_row = jnp.where(lax.broadcasted_iota(jnp.int32, (DIFF_LROWS, kb), 0) == 0, 1.0, 0.0).astype(BF16)

    def build(jb, carry):
        r0 = pl.multiple_of(jb * kb, kb)
        kk = k_ref[pl.ds(r0, kb), :].astype(F32)
        k1 = jnp.where(lane_k < DIFF_HD, kk, bias1)
        k2 = jnp.where(lane_k >= DIFF_HD, kk, bias2)
        kaug_ref[0, jb] = k1.astype(BF16)
        kaug_ref[1, jb] = k2.astype(BF16)
        vt_ref[jb, pl.ds(0, LANES), :] = v_ref[pl.ds(r0, kb), :].astype(F32).T.astype(BF16)
        vt_ref[jb, pl.ds(LANES, DIFF_LROWS), :] = ones_row
        return carry

    lax.fori_loop(0, seq // kb, build, 0)

    lam = (jnp.exp(jnp.sum(lq1_ref[...] * lk1_ref[...], axis=-1, keepdims=True))
           - jnp.exp(jnp.sum(lq2_ref[...] * lk2_ref[...], axis=-1, keepdims=True)) + lam_init)
    lane_q = lax.broadcasted_iota(jnp.int32, (qb, LANES), 1)
    one_q = jnp.ones((qb, LANES), F32)
    zero_q = jnp.zeros((qb, LANES), F32)
    qpos = lax.broadcasted_iota(jnp.int32, (1, qb), 1).astype(F32)
    causal = (lax.broadcasted_iota(jnp.int32, (qb, qb), 0)
              <= lax.broadcasted_iota(jnp.int32, (qb, qb), 1))
    bufs = (ta_ref, tb_ref)

    def scores(q_t, jb, nk, t_ref, diag):
        mx = []
        for c in range(2):
            t = jnp.dot(kaug_ref[c, jb, pl.ds(0, nk), :], q_t[c], preferred_element_type=F32)
            if diag:
                tri = jnp.where(causal, t[nk - qb:, :], NEG)
                t_ref[c, pl.ds(nk - qb, qb), :] = tri
                m = jnp.max(tri, axis=0, keepdims=True)
                if nk > qb:
                    t_ref[c, pl.ds(0, nk - qb), :] = t[:nk - qb, :]
                    m = jnp.maximum(m, jnp.max(t[:nk - qb, :], axis=0, keepdims=True))
            else:
                t_ref[c] = t
                m = jnp.max(t, axis=0, keepdims=True)
            mx.append(m)
        return tuple(mx)

    def accumulate(strip, jb, nk, dist0, t_ref, mx, state):
        u = slope * (jnp.asarray(dist0, F32) + qpos)
        new_state = []
        for c in range(2):
            m_new = mx[c] - u if state is None else jnp.maximum(state[c], mx[c] - u)
            p = jnp.exp2(t_ref[c, pl.ds(0, nk), :] - (m_new + u))
            pv = jnp.dot(vt_ref[jb, :, pl.ds(0, nk)], p.astype(BF16), preferred_element_type=F32)
            if state is None:
                acc_ref[strip, c] = pv
            else:
                acc_ref[strip, c] = jnp.exp2(state[c] - m_new) * acc_ref[strip, c] + pv
            new_state.append(m_new)
        return tuple(new_state)

    def super_block(sb, carry):
        q_t = []
        for s in range(per):
            i0 = pl.multiple_of(sb * kb + s * qb, qb)
            qq = q_ref[pl.ds(i0, qb), :].astype(F32)
            q1 = jnp.where(lane_q < DIFF_HD, qq, jnp.where(lane_q < DIFF_HD + 3, one_q, zero_q))
            q2 = jnp.where(lane_q >= DIFF_HD, qq, jnp.where(lane_q < 3, one_q, zero_q))
            q_t.append((q1.T.astype(BF16), q2.T.astype(BF16)))

        state = []
        mx = scores(q_t[0], sb, qb, bufs[0], True)
        for s in range(per):
            if s + 1 < per:
                mx_next = scores(q_t[s + 1], sb, (s + 2) * qb, bufs[(s + 1) % 2], True)
            else:
                mx_next = scores(q_t[0], jnp.maximum(sb - 1, 0), kb, bufs[per % 2], False)
            state.append(accumulate(s, sb, (s + 1) * qb, s * qb, bufs[s % 2], mx, None))
            mx = mx_next

        def full_tile(n, carry):
            mx, state = carry
            jb = sb - 1 - n
            dist = (n + 1) * kb
            state = list(state)
            for s in range(per):
                if s + 1 < per:
                    mx_next = scores(q_t[s + 1], jb, kb, bufs[(per + s + 1) % 2], False)
                else:
                    mx_next = scores(q_t[0], jnp.maximum(jb - 1, 0), kb, bufs[per % 2], False)
                state[s] = accumulate(s, jb, kb, dist + s * qb, bufs[(per + s) % 2], mx, state[s])
                mx = mx_next
            return mx, tuple(state)

        _, state = lax.fori_loop(0, sb, full_tile, (mx, tuple(state)))

        for s in range(per):
            l1 = acc_ref[s, 0, pl.ds(LANES, 1), :]
            l2 = acc_ref[s, 1, pl.ds(LANES, 1), :]
            o = (acc_ref[s, 0, pl.ds(0, LANES), :] * (1.0 / l1)
                 - lam * (acc_ref[s, 1, pl.ds(0, LANES), :] * (1.0 / l2)))
            y = o * lax.rsqrt(jnp.mean(o * o, axis=0, keepdims=True) + EPS)
            i0 = pl.multiple_of(sb * kb + s * qb, qb)
            o_ref[pl.ds(i0, qb), :] = (y.T * subg_ref[...] * (1.0 - lam_init)).astype(o_ref.dtype)
        return carry

    assert per % 2 == 0
    lax.fori_loop(0, seq // kb, super_block, 0)


def _diff_attn(qkv, lq1, lk1, lq2, lk2, subg, *, lam_init):
    b, s, _ = qkv.shape
    qb, kb = DIFF_QB, DIFF_KB
    small = lambda w: pl.BlockSpec((1, w), lambda bi, h: (0, 0))
    return pl.pallas_call(
        functools.partial(_diff_kernel, seq=s, lam_init=lam_init),
        grid=(b, DIFF_HEADS),
        in_specs=[pl.BlockSpec(memory_space=pltpu.SMEM),
                  pl.BlockSpec((None, s, LANES), lambda bi, h: (bi, 0, h)),
                  pl.BlockSpec((None, s, LANES), lambda bi, h: (bi, 0, COLBLK_DK + h)),
                  pl.BlockSpec((None, s, LANES), lambda bi, h: (bi, 0, COLBLK_DV + h)),
                  small(DIFF_HD), small(DIFF_HD), small(DIFF_HD), small(DIFF_HD), small(2 * DIFF_HD)],
        out_specs=pl.BlockSpec((None, s, LANES), lambda bi, h: (bi, 0, h)),
        out_shape=jax.ShapeDtypeStruct((b, s, DIFF_W), BF16),
        scratch_shapes=[pltpu.VMEM((2, s // kb, kb, LANES), BF16),
                        pltpu.VMEM((s // kb, LANES + DIFF_LROWS, kb), BF16),
                        pltpu.VMEM((kb // qb, 2, LANES + DIFF_LROWS, qb), F32),
                        pltpu.VMEM((2, kb, qb), F32),
                        pltpu.VMEM((2, kb, qb), F32)],
        compiler_params=_cparams(("parallel", "parallel"), DIFF_VMEM),
        name="diff_attn",
    )(_alibi_slopes(DIFF_HEADS), qkv, qkv, qkv,
      lq1.reshape(1, -1), lk1.reshape(1, -1), lq2.reshape(1, -1), lk2.reshape(1, -1), subg.reshape(1, -1))


def _dil_kernel(slopes_ref, *refs):
    ngrp = len(DIL_GROUPS)
    in_refs = refs[:5 * ngrp]
    o_ref, qf_ref, kf_ref, vf_ref, qy_ref, ky_ref, vy_ref, og_ref, lg_ref = refs[5 * ngrp:]
    ch, blk, sub = DIL_CHUNK, DIL_BLK, DIL_SUBSTRIDE
    c = pl.program_id(1)
    slope = slopes_ref[pl.program_id(2)] * LOG2E

    qi = lax.broadcasted_iota(jnp.int32, (blk, 2 * blk), 0)
    ki = lax.broadcasted_iota(jnp.int32, (blk, 2 * blk), 1)
    step = blk + qi - ki
    bias = jnp.where((step >= 0) & (step <= blk), -slope * step.astype(F32), NEG)
    seq_start = jnp.where(ki < blk, jnp.where(c == 0, NEG, 0.0), 0.0)

    def attend(q, kk, vv, first_block):
        s = lax.dot_general(q, kk, (((1,), (1,)), ((), ())), preferred_element_type=F32) + bias
        if first_block:
            s = s + seq_start
        m = jnp.max(s, axis=-1, keepdims=True)
        p = jnp.exp2(s - m)
        den = jnp.sum(p, axis=-1, keepdims=True)
        o = jnp.dot(p.astype(BF16), vv, preferred_element_type=F32) * (1.0 / den)
        return o, jnp.broadcast_to(m + jnp.log2(den), (blk, LANES))

    def run_units(unit):
        for u in range(ch // blk):
            unit(u)

    for g, (window, dil) in enumerate(DIL_GROUPS):
        q_ref, k_ref, v_ref, kp_ref, vp_ref = in_refs[5 * g:5 * g + 5]
        span = blk * dil
        if dil == 1:
            def unit(n, g=g, q_ref=q_ref, k_ref=k_ref, v_ref=v_ref, kp_ref=kp_ref, vp_ref=vp_ref):
                q = q_ref[pl.ds(n * blk, blk), :]
                if n == 0:
                    kk = jnp.concatenate([kp_ref[...], k_ref[pl.ds(0, blk), :]], axis=0)
                    vv = jnp.concatenate([vp_ref[...], v_ref[pl.ds(0, blk), :]], axis=0)
                else:
                    kk = k_ref[pl.ds((n - 1) * blk, 2 * blk), :]
                    vv = v_ref[pl.ds((n - 1) * blk, 2 * blk), :]
                o, lse = attend(q, kk, vv, n == 0)
                og_ref[g, pl.ds(n * blk, blk), :] = o
                lg_ref[g, pl.ds(n * blk, blk), :] = lse

            run_units(unit)
            continue

        qf_ref[...] = q_ref[...].astype(F32)
        kf_ref[pl.ds(ch, ch), :] = k_ref[...].astype(F32)
        vf_ref[pl.ds(ch, ch), :] = v_ref[...].astype(F32)
        kf_ref[pl.ds(ch - span, span), :] = kp_ref[...].astype(F32)
        vf_ref[pl.ds(ch - span, span), :] = vp_ref[...].astype(F32)
        two_stage = dil > sub
        if two_stage:
            assert span == ch and dil % sub == 0
            for r0 in range(sub):
                qy_ref[r0] = qf_ref[pl.ds(r0, ch // sub, stride=sub), :]
                ky_ref[r0] = kf_ref[pl.ds(r0, 2 * ch // sub, stride=sub), :]
                vy_ref[r0] = vf_ref[pl.ds(r0, 2 * ch // sub, stride=sub), :]

        def unit(uidx, g=g, dil=dil, span=span, two_stage=two_stage):
            n_local, r = divmod(uidx, dil)
            q0 = n_local * span + r
            k0 = ch - span + q0
            if two_stage:
                r1, r0 = divmod(r, sub)
                q = qy_ref[r0, pl.ds(r1, blk, stride=dil // sub), :]
                kk = ky_ref[r0, pl.ds(r1, 2 * blk, stride=dil // sub), :]
                vv = vy_ref[r0, pl.ds(r1, 2 * blk, stride=dil // sub), :]
            else:
                q = qf_ref[pl.ds(q0, blk, stride=dil), :]
                kk = kf_ref[pl.ds(k0, 2 * blk, stride=dil), :]
                vv = vf_ref[pl.ds(k0, 2 * blk, stride=dil), :]
            o, lse = attend(q.astype(BF16), kk.astype(BF16), vv.astype(BF16), n_local == 0)
            og_ref[g, pl.ds(q0, blk, stride=dil), :] = o
            lg_ref[g, pl.ds(q0, blk, stride=dil), :] = lse

        run_units(unit)

    rows = 256
    for r0 in range(0, ch, rows):
        ls = [lg_ref[g, pl.ds(r0, rows), :] for g in range(ngrp)]
        top = functools.reduce(jnp.maximum, ls)
        ws = [jnp.exp2(x - top) for x in ls]
        num = sum(w * og_ref[g, pl.ds(r0, rows), :] for g, w in enumerate(ws))
        o_ref[pl.ds(r0, rows), :] = (num * (1.0 / sum(ws))).astype(o_ref.dtype)


def _dil_attn(qkv):
    b, s, _ = qkv.shape
    ch, blk = DIL_CHUNK, DIL_BLK
    in_specs = [pl.BlockSpec(memory_space=pltpu.SMEM)]
    args = [_alibi_slopes(DIL_HEADS)]
    for g, (window, dil) in enumerate(DIL_GROUPS):
        span = blk * dil
        per = ch // span
        col = g * DIL_HEADS

        def cur(base, col=col):
            return pl.BlockSpec((None, ch, LANES), lambda bi, c, h: (bi, c, base + col + h))

        def prev(base, col=col, span=span, per=per):
            return pl.BlockSpec((None, span, LANES),
                                lambda bi, c, h: (bi, jnp.maximum(c * per - 1, 0), base + col + h))

        in_specs += [cur(COLBLK_SQ), cur(COLBLK_SK), cur(COLBLK_SV), prev(COLBLK_SK), prev(COLBLK_SV)]
        args += [qkv] * 5
    return pl.pallas_call(
        _dil_kernel,
        grid=(b, s // ch, DIL_HEADS),
        in_specs=in_specs,
        out_specs=pl.BlockSpec((None, ch, LANES), lambda bi, c, h: (bi, c, h)),
        out_shape=jax.ShapeDtypeStruct((b, s, DIL_OUT), BF16),
        scratch_shapes=[pltpu.VMEM((ch, LANES), F32),
                        pltpu.VMEM((2 * ch, LANES), F32),
                        pltpu.VMEM((2 * ch, LANES), F32),
                        pltpu.VMEM((DIL_SUBSTRIDE, ch // DIL_SUBSTRIDE, LANES), F32),
                        pltpu.VMEM((DIL_SUBSTRIDE, 2 * ch // DIL_SUBSTRIDE, LANES), F32),
                        pltpu.VMEM((DIL_SUBSTRIDE, 2 * ch // DIL_SUBSTRIDE, LANES), F32),
                        pltpu.VMEM((len(DIL_GROUPS), ch, LANES), F32),
                        pltpu.VMEM((len(DIL_GROUPS), ch, LANES), F32)],
        compiler_params=_cparams(("parallel", "parallel", "parallel"), DIL_VMEM),
        name="dil_attn",
    )(*args)


def _mix_kernel(od_ref, ol_ref, gd_ref, gl_ref, x_ref, wbd_ref, wbl_ref, wo_ref, o_ref):
    yd = jnp.dot(od_ref[...], wbd_ref[...], preferred_element_type=F32)
    yl = jnp.dot(ol_ref[...], wbl_ref[...], preferred_element_type=F32)
    y = gd_ref[...] * yd + gl_ref[...] * yl
    o_ref[...] = x_ref[...] + jnp.dot(y.astype(BF16), wo_ref[...], preferred_element_type=F32)


def _mix_out(od, ol, gates, x2d, wbd, wbl, wo):
    t, d = x2d.shape
    rb = MIX_RB
    return pl.pallas_call(
        _mix_kernel,
        grid=(t // rb,),
        in_specs=[pl.BlockSpec((rb, DIFF_W), lambda i: (i, 0)),
                  pl.BlockSpec((rb, DIL_OUT), lambda i: (i, 0)),
                  pl.BlockSpec((rb, d), lambda i: (i, 0)),
                  pl.BlockSpec((rb, d), lambda i: (i, 1)),
                  pl.BlockSpec((rb, d), lambda i: (i, 0)),
                  _resident(wbd.shape), _resident(wbl.shape), _resident(wo.shape)],
        out_specs=pl.BlockSpec((rb, d), lambda i: (i, 0)),
        out_shape=jax.ShapeDtypeStruct((t, d), F32),
        compiler_params=_cparams(("parallel",), MIX_VMEM),
        name="mix_out",
    )(od, ol, gates, gates, x2d, wbd, wbl, wo)


def _memkv_kernel(mem_ref, g_ref, w_ref, kt_ref, v_ref):
    d = mem_ref.shape[-1]
    mn = _rms(mem_ref[...], g_ref[...]).astype(BF16)
    kv = jnp.dot(mn, w_ref[...], preferred_element_type=F32)
    kt_ref[...] = kv[:, :d].T.astype(BF16)
    v_ref[...] = kv[:, d:].astype(BF16)


def _mem_kv(mem, g, w):
    b, n, d = mem.shape
    return pl.pallas_call(
        _memkv_kernel,
        grid=(b,),
        in_specs=[pl.BlockSpec((None, n, d), lambda bi: (bi, 0, 0)),
                  pl.BlockSpec((1, d), lambda bi: (0, 0)),
                  pl.BlockSpec((d, 2 * d), lambda bi: (0, 0))],
        out_specs=[pl.BlockSpec((None, d, n), lambda bi: (bi, 0, 0)),
                   pl.BlockSpec((None, n, d), lambda bi: (bi, 0, 0))],
        out_shape=[jax.ShapeDtypeStruct((b, d, n), BF16), jax.ShapeDtypeStruct((b, n, d), BF16)],
        compiler_params=_cparams(("parallel",), MEMKV_VMEM),
        name="mem_kv",
    )(mem, g.reshape(1, d), w)


def _cross_kernel(x_ref, g_ref, wq_ref, kt_ref, v_ref, wo_ref, o_ref):
    d = x_ref.shape[-1]
    hd = d // X_HEADS
    x = x_ref[...]
    hq = _rms(x, g_ref[...]).astype(BF16)
    q = (jnp.dot(hq, wq_ref[...], preferred_element_type=F32) * (hd ** -0.5)).astype(BF16)
    outs = []
    for h in range(X_HEADS):
        sl = slice(h * hd, (h + 1) * hd)
        s = jnp.dot(q[:, sl], kt_ref[sl, :], preferred_element_type=F32)
        m = jnp.max(s, axis=-1, keepdims=True)
        p = jnp.exp(s - m)
        den = jnp.sum(p, axis=-1, keepdims=True)
        outs.append(jnp.dot(p.astype(BF16), v_ref[:, sl], preferred_element_type=F32) * (1.0 / den))
    o = jnp.concatenate(outs, axis=-1).astype(BF16)
    o_ref[...] = x + jnp.dot(o, wo_ref[...], preferred_element_type=F32)


def _cross_attn(x1, g, wq, kt, v, wo):
    b, s, d = x1.shape
    rb = CROSS_RB
    n = v.shape[1]
    return pl.pallas_call(
        _cross_kernel,
        grid=(b, s // rb),
        in_specs=[pl.BlockSpec((None, rb, d), lambda bi, i: (bi, i, 0)),
                  _resident((1, d)), _resident((d, d)),
                  pl.BlockSpec((None, d, n), lambda bi, i: (bi, 0, 0)),
                  pl.BlockSpec((None, n, d), lambda bi, i: (bi, 0, 0)),
                  _resident((d, d))],
        out_specs=pl.BlockSpec((None, rb, d), lambda bi, i: (bi, i, 0)),
        out_shape=jax.ShapeDtypeStruct((b, s, d), F32),
        compiler_params=_cparams(("parallel", "parallel"), CROSS_VMEM),
        name="cross_attn",
    )(x1, g.reshape(1, d), wq, kt, v, wo)


def _ffn_kernel(x_ref, g_ref, wu_ref, cw_ref, cb_ref, wd_ref, fg_ref, o_ref, abuf_ref, carry_ref,
                *, blocks_per_seq, final_norm):
    rb = x_ref.shape[0]
    ff = wd_ref.shape[0]
    halo = CONV_HALO
    x = x_ref[...]
    h = _rms(x, g_ref[...]).astype(BF16)
    first = (pl.program_id(0) % blocks_per_seq) == 0
    acc = None
    for c0 in range(0, ff, FFN_CHUNK):
        fc = min(FFN_CHUNK, ff - c0)
        cols = pl.ds(c0, fc)
        a = jnp.dot(h, wu_ref[:, cols], preferred_element_type=F32)
        bgate = jnp.dot(h, wu_ref[:, pl.ds(ff + c0, fc)], preferred_element_type=F32)
        prev = carry_ref[:, cols]
        abuf_ref[pl.ds(0, halo), cols] = jnp.where(first, jnp.zeros_like(prev), prev)
        abuf_ref[pl.ds(halo, rb), cols] = a
        carry_ref[:, cols] = a[rb - halo:, :]
        cw = cw_ref[:, cols]
        y = (cw[2:3, :] * a + cw[1:2, :] * abuf_ref[pl.ds(halo - 1, rb), cols]
             + cw[0:1, :] * abuf_ref[pl.ds(halo - 2, rb), cols] + cb_ref[:, cols])
        act = 0.5 * y * (1.0 + lax.erf(y * (2.0 ** -0.5)))
        part = jnp.dot((act * bgate).astype(BF16), wd_ref[cols, :], preferred_element_type=F32)
        acc = part if acc is None else acc + part
    y = x + acc
    o_ref[...] = _rms(y, fg_ref[...]) if final_norm else y


def _ffn(x2d, g, w_up, cw, cb, w_down, fg, *, seq, final_norm):
    t, d = x2d.shape
    ff = w_down.shape[0]
    rb = FFN_RB
    blocks_per_seq = seq // rb
    assert ff % LANES == 0 and rb >= CONV_HALO and seq % rb == 0
    return pl.pallas_call(
        functools.partial(_ffn_kernel, blocks_per_seq=blocks_per_seq, final_norm=final_norm),
        grid=(t // rb,),
        in_specs=[pl.BlockSpec((rb, d), lambda i: (i, 0)),
                  _resident((1, d)), _resident(w_up.shape), _resident((CONV_W, ff)), _resident((1, ff)),
                  _resident(w_down.shape), _resident((1, d))],
        out_specs=pl.BlockSpec((rb, d), lambda i: (i, 0)),
        out_shape=jax.ShapeDtypeStruct((t, d), F32),
        scratch_shapes=[pltpu.VMEM((rb + CONV_HALO, ff), F32),
                        pltpu.VMEM((CONV_HALO, ff), F32)],
        compiler_params=_cparams(("arbitrary",), FFN_VMEM),
        name="ffn",
    )(x2d, g.reshape(1, d), w_up, cw, cb.reshape(1, ff), w_down, fg.reshape(1, d))


def kernel(x, mem, mix_norm_g, w_in, b_gate, lam_q1, lam_k1, lam_q2, lam_k2, diff_subln_g,
           w_branch_diff, w_branch_dil, w_out, cross_norm_g, mem_norm_g, w_cq, w_ckv, w_co,
           ffn_norm_g, w_up, conv_w, conv_b, w_down, final_norm_g):
    b, s, d = x.shape
    depth = w_in.shape[0]
    t = b * s
    assert s % DIL_CHUNK == 0 and s % DIFF_KB == 0 and DIFF_KB % DIFF_QB == 0 and d == DIFF_W
    assert t % INPROJ_RB == 0 and t % MIX_RB == 0 and s % CROSS_RB == 0
    xf = x.reshape(t, d)
    for l in range(depth):
        lam_init = 0.8 - 0.6 * math.exp(-0.3 * l)
        qkv, gates = _inproj(xf, mix_norm_g[l], w_in[l].astype(BF16), b_gate[l])
        qkv3 = qkv.reshape(b, s, QKV_W)
        o_diff = _diff_attn(qkv3, lam_q1[l], lam_k1[l], lam_q2[l], lam_k2[l], diff_subln_g[l],
                            lam_init=lam_init)
        o_dil = _dil_attn(qkv3)
        x1 = _mix_out(o_diff.reshape(t, DIFF_W), o_dil.reshape(t, DIL_OUT), gates, xf,
                      w_branch_diff[l].astype(BF16), w_branch_dil[l].astype(BF16),
                      w_out[l].astype(BF16))
        mk_t, mv = _mem_kv(mem, mem_norm_g[l], w_ckv[l].astype(BF16))
        x2 = _cross_attn(x1.reshape(b, s, d), cross_norm_g[l], w_cq[l].astype(BF16), mk_t, mv,
                         w_co[l].astype(BF16))
        xf = _ffn(x2.reshape(t, d), ffn_norm_g[l], w_up[l].astype(BF16), conv_w[l], conv_b[l],
                  w_down[l].astype(BF16), final_norm_g, seq=s, final_norm=(l == depth - 1))
    return xf.reshape(b, s, d)
```

```python
import functools
import math

import jax
import jax.numpy as jnp
from jax import lax
from jax.experimental import pallas as pl
from jax.experimental.pallas import tpu as pltpu

F32 = jnp.float32
BF16 = jnp.bfloat16

EPS = 1e-5
NEG = -1e30
LOG2E = math.log2(math.e)
LANES = 128

DIFF_HEADS = 8
DIFF_HD = 64
DIFF_W = DIFF_HEADS * 2 * DIFF_HD
DIFF_QB = 256
DIFF_KB = 1024
DIFF_LROWS = 16
DIL_GROUPS = ((128, 1), (512, 4), (2048, 16))
DIL_HEADS = 4
DIL_HD = 128
DIL_W = len(DIL_GROUPS) * DIL_HEADS * DIL_HD
DIL_OUT = DIL_HEADS * DIL_HD
DIL_BLK = 128
DIL_CHUNK = 2048
DIL_SUBSTRIDE = 4
X_HEADS = 4
CONV_W = 3
CONV_HALO = 8
FFN_CHUNK = 1536

INPROJ_RB, INPROJ_VMEM = 512, 60
DIFF_VMEM = 56
DIL_VMEM = 48
MIX_RB, MIX_VMEM = 1024, 56
MEMKV_VMEM = 32
CROSS_RB, CROSS_VMEM = 1024, 48
FFN_RB, FFN_VMEM = 512, 60

QKV_W = 3 * DIFF_W + 3 * DIL_W
INPROJ_NC = 512
COLBLK_DK = DIFF_W // LANES
COLBLK_DV = 2 * DIFF_W // LANES
COLBLK_SQ = 3 * DIFF_W // LANES
COLBLK_SK = COLBLK_SQ + DIL_W // LANES
COLBLK_SV = COLBLK_SK + DIL_W // LANES


def _alibi_slopes(n):
    return jnp.exp2(-8.0 * jnp.arange(1, n + 1, dtype=F32) / n)


def _cparams(sem, vmem_mb):
    return pltpu.CompilerParams(dimension_semantics=sem, vmem_limit_bytes=vmem_mb * 1024 * 1024)


def _rms(x, g):
    return x * lax.rsqrt(jnp.mean(x * x, axis=-1, keepdims=True) + EPS) * g


def _resident(shape):
    return pl.BlockSpec(shape, lambda *_: (0,) * len(shape), pipeline_mode=pl.Buffered(1))


def _inproj_kernel(x_ref, g_ref, w_ref, b_ref, qkv_ref, gate_ref):
    h = _rms(x_ref[...], g_ref[...]).astype(BF16)
    nq, ng, nc = qkv_ref.shape[1], gate_ref.shape[1], INPROJ_NC
    for c0 in range(0, nq, nc):
        acc = jnp.dot(h, w_ref[:, pl.ds(c0, nc)], preferred_element_type=F32)
        if c0 < DIFF_W:
            acc = acc * (LOG2E * DIFF_HD ** -0.5)
        elif 3 * DIFF_W <= c0 < 3 * DIFF_W + DIL_W:
            acc = acc * (LOG2E * DIL_HD ** -0.5)
        qkv_ref[:, pl.ds(c0, nc)] = acc.astype(qkv_ref.dtype)
    for c0 in range(0, ng, nc):
        acc = jnp.dot(h, w_ref[:, pl.ds(nq + c0, nc)], preferred_element_type=F32)
        gate_ref[:, pl.ds(c0, nc)] = jax.nn.sigmoid(acc + b_ref[:, pl.ds(c0, nc)])


def _inproj(x2d, g, w, b_gate):
    t, d = x2d.shape
    rb = INPROJ_RB
    ng = b_gate.shape[0]
    nq = w.shape[1] - ng
    assert DIFF_W % INPROJ_NC == 0 and DIL_W % INPROJ_NC == 0 and ng % INPROJ_NC == 0 and nq == QKV_W
    return pl.pallas_call(
        _inproj_kernel,
        grid=(t // rb,),
        in_specs=[pl.BlockSpec((rb, d), lambda i: (i, 0)),
                  _resident((1, d)), _resident(w.shape), _resident((1, ng))],
        out_specs=[pl.BlockSpec((rb, nq), lambda i: (i, 0)),
                   pl.BlockSpec((rb, ng), lambda i: (i, 0))],
        out_shape=[jax.ShapeDtypeStruct((t, nq), BF16), jax.ShapeDtypeStruct((t, ng), F32)],
        compiler_params=_cparams(("parallel",), INPROJ_VMEM),
        name="inproj",
    )(x2d, g.reshape(1, d), w, b_gate.reshape(1, ng))


def _diff_kernel(slopes_ref, q_ref, k_ref, v_ref, lq1_ref, lk1_ref, lq2_ref, lk2_ref, subg_ref,
                 o_ref, kaug_ref, vt_ref, acc_ref, ta_ref, tb_ref, *, seq, lam_init):
    qb, kb = DIFF_QB, DIFF_KB
    per = kb // qb
    slope = slopes_ref[pl.program_id(1)] * LOG2E

    lane_k = lax.broadcasted_iota(jnp.int32, (kb, LANES), 1)
    val = slope * lax.broadcasted_iota(jnp.int32, (kb, LANES), 0).astype(F32)
    hi = val.astype(BF16).astype(F32)
    mid = (val - hi).astype(BF16).astype(F32)
    lo = val - hi - mid
    zero_k = jnp.zeros((kb, LANES), F32)

    def bias_cols(first):
        return jnp.where(lane_k == first, hi,
                         jnp.where(lane_k == first + 1, mid, jnp.where(lane_k == first + 2, lo, zero_k)))

    bias1, bias2 = bias_cols(DIFF_HD), bias_cols(0)
    ones_row = jnp.where(lax.broadcasted_iota(jnp.int32, (DIFF_LROWS, kb), 0) == 0, 1.0, 0.0).astype(BF16)

    def build(jb, carry):
        r0 = pl.multiple_of(jb * kb, kb)
        kk = k_ref[pl.ds(r0, kb), :].astype(F32)
        k1 = jnp.where(lane_k < DIFF_HD, kk, bias1)
        k2 = jnp.where(lane_k >= DIFF_HD, kk, bias2)
        kaug_ref[0, jb] = k1.astype(BF16)
        kaug_ref[1, jb] = k2.astype(BF16)
        vt_ref[jb, pl.ds(0, LANES), :] = v_ref[pl.ds(r0, kb), :].astype(F32).T.astype(BF16)
        vt_ref[jb, pl.ds(LANES, DIFF_LROWS), :] = ones_row
        return carry

    lax.fori_loop(0, seq // kb, build, 0)

    lam = (jnp.exp(jnp.sum(lq1_ref[...] * lk1_ref[...], axis=-1, keepdims=True))
           - jnp.exp(jnp.sum(lq2_ref[...] * lk2_ref[...], axis=-1, keepdims=True)) + lam_init)
    lane_q = lax.broadcasted_iota(jnp.int32, (qb, LANES), 1)
    one_q = jnp.ones((qb, LANES), F32)
    zero_q = jnp.zeros((qb, LANES), F32)
    qpos = lax.broadcasted_iota(jnp.int32, (1, qb), 1).astype(F32)
    causal = (lax.broadcasted_iota(jnp.int32, (qb, qb), 0)
              <= lax.broadcasted_iota(jnp.int32, (qb, qb), 1))
    bufs = (ta_ref, tb_ref)

    def scores(q_t, jb, nk, t_ref, diag):
        mx = []
        for c in range(2):
            t = jnp.dot(kaug_ref[c, jb, pl.ds(0, nk), :], q_t[c], preferred_element_type=F32)
            if diag:
                tri = jnp.where(causal, t[nk - qb:, :], NEG)
                t_ref[c, pl.ds(nk - qb, qb), :] = tri
                m = jnp.max(tri, axis=0, keepdims=True)
                if nk > qb:
                    t_ref[c, pl.ds(0, nk - qb), :] = t[:nk - qb, :]
                    m = jnp.maximum(m, jnp.max(t[:nk - qb, :], axis=0, keepdims=True))
            else:
                t_ref[c] = t
                m = jnp.max(t, axis=0, keepdims=True)
            mx.append(m)
        return tuple(mx)

    def accumulate(strip, jb, nk, dist0, t_ref, mx, state):
        u = slope * (jnp.asarray(dist0, F32) + qpos)
        new_state = []
        for c in range(2):
            m_new = mx[c] - u if state is None else jnp.maximum(state[c], mx[c] - u)
            p = jnp.exp2(t_ref[c, pl.ds(0, nk), :] - (m_new + u))
            pv = jnp.dot(vt_ref[jb, :, pl.ds(0, nk)], p.astype(BF16), preferred_element_type=F32)
            if state is None:
                acc_ref[strip, c] = pv
            else:
                acc_ref[strip, c] = jnp.exp2(state[c] - m_new) * acc_ref[strip, c] + pv
            new_state.append(m_new)
        return tuple(new_state)

    def finalize(sb_done):
        for s in range(per):
            l1 = acc_ref[s, 0, pl.ds(LANES, 1), :]
            l2 = acc_ref[s, 1, pl.ds(LANES, 1), :]
            o = (acc_ref[s, 0, pl.ds(0, LANES), :] * (1.0 / l1)
                 - lam * (acc_ref[s, 1, pl.ds(0, LANES), :] * (1.0 / l2)))
            y = o * lax.rsqrt(jnp.mean(o * o, axis=0, keepdims=True) + EPS)
            i0 = pl.multiple_of(sb_done * kb + s * qb, qb)
            o_ref[pl.ds(i0, qb), :] = (y.T * subg_ref[...] * (1.0 - lam_init)).astype(o_ref.dtype)

    def super_block(sb, carry):
        finalize(jnp.maximum(sb - 1, 0))
        q_t = []
        for s in range(per):
            i0 = pl.multiple_of(sb * kb + s * qb, qb)
            qq = q_ref[pl.ds(i0, qb), :].astype(F32)
            q1 = jnp.where(lane_q < DIFF_HD, qq, jnp.where(lane_q < DIFF_HD + 3, one_q, zero_q))
            q2 = jnp.where(lane_q >= DIFF_HD, qq, jnp.where(lane_q < 3, one_q, zero_q))
            q_t.append((q1.T.astype(BF16), q2.T.astype(BF16)))

        state = []
        mx = scores(q_t[0], sb, qb, bufs[0], True)
        for s in range(per):
            if s + 1 < per:
                mx_next = scores(q_t[s + 1], sb, (s + 2) * qb, bufs[(s + 1) % 2], True)
            else:
                mx_next = scores(q_t[0], jnp.maximum(sb - 1, 0), kb, bufs[per % 2], False)
            state.append(accumulate(s, sb, (s + 1) * qb, s * qb, bufs[s % 2], mx, None))
            mx = mx_next

        def full_tile(n, carry):
            mx, state = carry
            jb = sb - 1 - n
            dist = (n + 1) * kb
            state = list(state)
            for s in range(per):
                if s + 1 < per:
                    mx_next = scores(q_t[s + 1], jb, kb, bufs[(per + s + 1) % 2], False)
                else:
                    mx_next = scores(q_t[0], jnp.maximum(jb - 1, 0), kb, bufs[per % 2], False)
                state[s] = accumulate(s, jb, kb, dist + s * qb, bufs[(per + s) % 2], mx, state[s])
                mx = mx_next
            return mx, tuple(state)

        lax.fori_loop(0, sb, full_tile, (mx, tuple(state)))
        return carry

    assert per % 2 == 0
    acc_init = jnp.where(lax.broadcasted_iota(jnp.int32, (LANES + DIFF_LROWS, qb), 0) == LANES, 1.0, 0.0)
    for s in range(per):
        for c in range(2):
            acc_ref[s, c] = acc_init
    lax.fori_loop(0, seq // kb, super_block, 0)
    finalize(seq // kb - 1)


def _diff_attn(qkv, lq1, lk1, lq2, lk2, subg, *, lam_init):
    b, s, _ = qkv.shape
    qb, kb = DIFF_QB, DIFF_KB
    small = lambda w: pl.BlockSpec((1, w), lambda bi, h: (0, 0))
    return pl.pallas_call(
        functools.partial(_diff_kernel, seq=s, lam_init=lam_init),
        grid=(b, DIFF_HEADS),
        in_specs=[pl.BlockSpec(memory_space=pltpu.SMEM),
                  pl.BlockSpec((None, s, LANES), lambda bi, h: (bi, 0, h)),
                  pl.BlockSpec((None, s, LANES), lambda bi, h: (bi, 0, COLBLK_DK + h)),
                  pl.BlockSpec((None, s, LANES), lambda bi, h: (bi, 0, COLBLK_DV + h)),
                  small(DIFF_HD), small(DIFF_HD), small(DIFF_HD), small(DIFF_HD), small(2 * DIFF_HD)],
        out_specs=pl.BlockSpec((None, s, LANES), lambda bi, h: (bi, 0, h)),
        out_shape=jax.ShapeDtypeStruct((b, s, DIFF_W), BF16),
        scratch_shapes=[pltpu.VMEM((2, s // kb, kb, LANES), BF16),
                        pltpu.VMEM((s // kb, LANES + DIFF_LROWS, kb), BF16),
                        pltpu.VMEM((kb // qb, 2, LANES + DIFF_LROWS, qb), F32),
                        pltpu.VMEM((2, kb, qb), F32),
                        pltpu.VMEM((2, kb, qb), F32)],
        compiler_params=_cparams(("parallel", "parallel"), DIFF_VMEM),
        name="diff_attn",
    )(_alibi_slopes(DIFF_HEADS), qkv, qkv, qkv,
      lq1.reshape(1, -1), lk1.reshape(1, -1), lq2.reshape(1, -1), lk2.reshape(1, -1), subg.reshape(1, -1))


def _dil_kernel(slopes_ref, *refs):
    ngrp = len(DIL_GROUPS)
    in_refs = refs[:5 * ngrp]
    o_ref, qf_ref, kf_ref, vf_ref, qy_ref, ky_ref, vy_ref, og_ref, lg_ref = refs[5 * ngrp:]
    ch, blk, sub = DIL_CHUNK, DIL_BLK, DIL_SUBSTRIDE
    c = pl.program_id(1)
    slope = slopes_ref[pl.program_id(2)] * LOG2E

    qi = lax.broadcasted_iota(jnp.int32, (blk, 2 * blk), 0)
    ki = lax.broadcasted_iota(jnp.int32, (blk, 2 * blk), 1)
    step = blk + qi - ki
    bias = jnp.where((step >= 0) & (step <= blk), -slope * step.astype(F32), NEG)
    seq_start = jnp.where(ki < blk, jnp.where(c == 0, NEG, 0.0), 0.0)

    def attend(q, kk, vv, first_block):
        s = lax.dot_general(q, kk, (((1,), (1,)), ((), ())), preferred_element_type=F32) + bias
        if first_block:
            s = s + seq_start
        m = jnp.max(s, axis=-1, keepdims=True)
        p = jnp.exp2(s - m)
        den = jnp.sum(p, axis=-1, keepdims=True)
        o = jnp.dot(p.astype(BF16), vv, preferred_element_type=F32) * (1.0 / den)
        return o, jnp.broadcast_to(m + jnp.log2(den), (blk, LANES))

    def run_units(unit):
        for u in range(ch // blk):
            unit(u)

    for g, (window, dil) in enumerate(DIL_GROUPS):
        q_ref, k_ref, v_ref, kp_ref, vp_ref = in_refs[5 * g:5 * g + 5]
        span = blk * dil
        if dil == 1:
            def unit(n, g=g, q_ref=q_ref, k_ref=k_ref, v_ref=v_ref, kp_ref=kp_ref, vp_ref=vp_ref):
                q = q_ref[pl.ds(n * blk, blk), :]
                if n == 0:
                    kk = jnp.concatenate([kp_ref[...], k_ref[pl.ds(0, blk), :]], axis=0)
                    vv = jnp.concatenate([vp_ref[...], v_ref[pl.ds(0, blk), :]], axis=0)
                else:
                    kk = k_ref[pl.ds((n - 1) * blk, 2 * blk), :]
                    vv = v_ref[pl.ds((n - 1) * blk, 2 * blk), :]
                o, lse = attend(q, kk, vv, n == 0)
                og_ref[g, pl.ds(n * blk, blk), :] = o
                lg_ref[g, pl.ds(n * blk, blk), :] = lse

            run_units(unit)
            continue

        qf_ref[...] = q_ref[...].astype(F32)
        kf_ref[pl.ds(ch, ch), :] = k_ref[...].astype(F32)
        vf_ref[pl.ds(ch, ch), :] = v_ref[...].astype(F32)
        kf_ref[pl.ds(ch - span, span), :] = kp_ref[...].astype(F32)
        vf_ref[pl.ds(ch - span, span), :] = vp_ref[...].astype(F32)
        two_stage = dil > sub
        if two_stage:
            assert span == ch and dil % sub == 0
            for r0 in range(sub):
                qy_ref[r0] = qf_ref[pl.ds(r0, ch // sub, stride=sub), :]
                ky_ref[r0] = kf_ref[pl.ds(r0, 2 * ch // sub, stride=sub), :]
                vy_ref[r0] = vf_ref[pl.ds(r0, 2 * ch // sub, stride=sub), :]

        def unit(uidx, g=g, dil=dil, span=span, two_stage=two_stage):
            n_local, r = divmod(uidx, dil)
            q0 = n_local * span + r
            k0 = ch - span + q0
            if two_stage:
                r1, r0 = divmod(r, sub)
                q = qy_ref[r0, pl.ds(r1, blk, stride=dil // sub), :]
                kk = ky_ref[r0, pl.ds(r1, 2 * blk, stride=dil // sub), :]
                vv = vy_ref[r0, pl.ds(r1, 2 * blk, stride=dil // sub), :]
            else:
                q = qf_ref[pl.ds(q0, blk, stride=dil), :]
                kk = kf_ref[pl.ds(k0, 2 * blk, stride=dil), :]
                vv = vf_ref[pl.ds(k0, 2 * blk, stride=dil), :]
            o, lse = attend(q.astype(BF16), kk.astype(BF16), vv.astype(BF16), n_local == 0)
            og_ref[g, pl.ds(q0, blk, stride=dil), :] = o
            lg_ref[g, pl.ds(q0, blk, stride=dil), :] = lse

        run_units(unit)

    rows = 256
    for r0 in range(0, ch, rows):
        ls = [lg_ref[g, pl.ds(r0, rows), :] for g in range(ngrp)]
        top = functools.reduce(jnp.maximum, ls)
        ws = [jnp.exp2(x - top) for x in ls]
        num = sum(w * og_ref[g, pl.ds(r0, rows), :] for g, w in enumerate(ws))
        o_ref[pl.ds(r0, rows), :] = (num * (1.0 / sum(ws))).astype(o_ref.dtype)


def _dil_attn(qkv):
    b, s, _ = qkv.shape
    ch, blk = DIL_CHUNK, DIL_BLK
    in_specs = [pl.BlockSpec(memory_space=pltpu.SMEM)]
    args = [_alibi_slopes(DIL_HEADS)]
    for g, (window, dil) in enumerate(DIL_GROUPS):
        span = blk * dil
        per = ch // span
        col = g * DIL_HEADS

        def cur(base, col=col):
            return pl.BlockSpec((None, ch, LANES), lambda bi, c, h: (bi, c, base + col + h))

        def prev(base, col=col, span=span, per=per):
            return pl.BlockSpec((None, span, LANES),
                                lambda bi, c, h: (bi, jnp.maximum(c * per - 1, 0), base + col + h))

        in_specs += [cur(COLBLK_SQ), cur(COLBLK_SK), cur(COLBLK_SV), prev(COLBLK_SK), prev(COLBLK_SV)]
        args += [qkv] * 5
    return pl.pallas_call(
        _dil_kernel,
        grid=(b, s // ch, DIL_HEADS),
        in_specs=in_specs,
        out_specs=pl.BlockSpec((None, ch, LANES), lambda bi, c, h: (bi, c, h)),
        out_shape=jax.ShapeDtypeStruct((b, s, DIL_OUT), BF16),
        scratch_shapes=[pltpu.VMEM((ch, LANES), F32),
                        pltpu.VMEM((2 * ch, LANES), F32),
                        pltpu.VMEM((2 * ch, LANES), F32),
                        pltpu.VMEM((DIL_SUBSTRIDE, ch // DIL_SUBSTRIDE, LANES), F32),
                        pltpu.VMEM((DIL_SUBSTRIDE, 2 * ch // DIL_SUBSTRIDE, LANES), F32),
                        pltpu.VMEM((DIL_SUBSTRIDE, 2 * ch // DIL_SUBSTRIDE, LANES), F32),
                        pltpu.VMEM((len(DIL_GROUPS), ch, LANES), F32),
                        pltpu.VMEM((len(DIL_GROUPS), ch, LANES), F32)],
        compiler_params=_cparams(("parallel", "parallel", "parallel"), DIL_VMEM),
        name="dil_attn",
    )(*args)


def _mix_kernel(od_ref, ol_ref, gd_ref, gl_ref, x_ref, wbd_ref, wbl_ref, wo_ref, o_ref):
    yd = jnp.dot(od_ref[...], wbd_ref[...], preferred_element_type=F32)
    yl = jnp.dot(ol_ref[...], wbl_ref[...], preferred_element_type=F32)
    y = gd_ref[...] * yd + gl_ref[...] * yl
    o_ref[...] = x_ref[...] + jnp.dot(y.astype(BF16), wo_ref[...], preferred_element_type=F32)


def _mix_out(od, ol, gates, x2d, wbd, wbl, wo):
    t, d = x2d.shape
    rb = MIX_RB
    return pl.pallas_call(
        _mix_kernel,
        grid=(t // rb,),
        in_specs=[pl.BlockSpec((rb, DIFF_W), lambda i: (i, 0)),
                  pl.BlockSpec((rb, DIL_OUT), lambda i: (i, 0)),
                  pl.BlockSpec((rb, d), lambda i: (i, 0)),
                  pl.BlockSpec((rb, d), lambda i: (i, 1)),
                  pl.BlockSpec((rb, d), lambda i: (i, 0)),
                  _resident(wbd.shape), _resident(wbl.shape), _resident(wo.shape)],
        out_specs=pl.BlockSpec((rb, d), lambda i: (i, 0)),
        out_shape=jax.ShapeDtypeStruct((t, d), F32),
        compiler_params=_cparams(("parallel",), MIX_VMEM),
        name="mix_out",
    )(od, ol, gates, gates, x2d, wbd, wbl, wo)


def _memkv_kernel(mem_ref, g_ref, w_ref, kt_ref, v_ref):
    d = mem_ref.shape[-1]
    mn = _rms(mem_ref[...], g_ref[...]).astype(BF16)
    kv = jnp.dot(mn, w_ref[...], preferred_element_type=F32)
    kt_ref[...] = kv[:, :d].T.astype(BF16)
    v_ref[...] = kv[:, d:].astype(BF16)


def _mem_kv(mem, g, w):
    b, n, d = mem.shape
    return pl.pallas_call(
        _memkv_kernel,
        grid=(b,),
        in_specs=[pl.BlockSpec((None, n, d), lambda bi: (bi, 0, 0)),
                  pl.BlockSpec((1, d), lambda bi: (0, 0)),
                  pl.BlockSpec((d, 2 * d), lambda bi: (0, 0))],
        out_specs=[pl.BlockSpec((None, d, n), lambda bi: (bi, 0, 0)),
                   pl.BlockSpec((None, n, d), lambda bi: (bi, 0, 0))],
        out_shape=[jax.ShapeDtypeStruct((b, d, n), BF16), jax.ShapeDtypeStruct((b, n, d), BF16)],
        compiler_params=_cparams(("parallel",), MEMKV_VMEM),
        name="mem_kv",
    )(mem, g.reshape(1, d), w)


def _cross_kernel(x_ref, g_ref, wq_ref, kt_ref, v_ref, wo_ref, o_ref):
    d = x_ref.shape[-1]
    hd = d // X_HEADS
    x = x_ref[...]
    hq = _rms(x, g_ref[...]).astype(BF16)
    q = (jnp.dot(hq, wq_ref[...], preferred_element_type=F32) * (hd ** -0.5)).astype(BF16)
    outs = []
    for h in range(X_HEADS):
        sl = slice(h * hd, (h + 1) * hd)
        s = jnp.dot(q[:, sl], kt_ref[sl, :], preferred_element_type=F32)
        m = jnp.max(s, axis=-1, keepdims=True)
        p = jnp.exp(s - m)
        den = jnp.sum(p, axis=-1, keepdims=True)
        outs.append(jnp.dot(p.astype(BF16), v_ref[:, sl], preferred_element_type=F32) * (1.0 / den))
    o = jnp.concatenate(outs, axis=-1).astype(BF16)
    o_ref[...] = x + jnp.dot(o, wo_ref[...], preferred_element_type=F32)


def _cross_attn(x1, g, wq, kt, v, wo):
    b, s, d = x1.shape
    rb = CROSS_RB
    n = v.shape[1]
    return pl.pallas_call(
        _cross_kernel,
        grid=(b, s // rb),
        in_specs=[pl.BlockSpec((None, rb, d), lambda bi, i: (bi, i, 0)),
                  _resident((1, d)), _resident((d, d)),
                  pl.BlockSpec((None, d, n), lambda bi, i: (bi, 0, 0)),
                  pl.BlockSpec((None, n, d), lambda bi, i: (bi, 0, 0)),
                  _resident((d, d))],
        out_specs=pl.BlockSpec((None, rb, d), lambda bi, i: (bi, i, 0)),
        out_shape=jax.ShapeDtypeStruct((b, s, d), F32),
        compiler_params=_cparams(("parallel", "parallel"), CROSS_VMEM),
        name="cross_attn",
    )(x1, g.reshape(1, d), wq, kt, v, wo)


def _ffn_kernel(x_ref, g_ref, wu_ref, cw_ref, cb_ref, wd_ref, fg_ref, o_ref, abuf_ref, carry_ref,
                *, blocks_per_seq, final_norm):
    rb = x_ref.shape[0]
    ff = wd_ref.shape[0]
    halo = CONV_HALO
    x = x_ref[...]
    h = _rms(x, g_ref[...]).astype(BF16)
    first = (pl.program_id(0) % blocks_per_seq) == 0
    acc = None
    for c0 in range(0, ff, FFN_CHUNK):
        fc = min(FFN_CHUNK, ff - c0)
        cols = pl.ds(c0, fc)
        a = jnp.dot(h, wu_ref[:, cols], preferred_element_type=F32)
        bgate = jnp.dot(h, wu_ref[:, pl.ds(ff + c0, fc)], preferred_element_type=F32)
        prev = carry_ref[:, cols]
        abuf_ref[pl.ds(0, halo), cols] = jnp.where(first, jnp.zeros_like(prev), prev)
        abuf_ref[pl.ds(halo, rb), cols] = a
        carry_ref[:, cols] = a[rb - halo:, :]
        cw = cw_ref[:, cols]
        y = (cw[2:3, :] * a + cw[1:2, :] * abuf_ref[pl.ds(halo - 1, rb), cols]
             + cw[0:1, :] * abuf_ref[pl.ds(halo - 2, rb), cols] + cb_ref[:, cols])
        act = 0.5 * y * (1.0 + lax.erf(y * (2.0 ** -0.5)))
        part = jnp.dot((act * bgate).astype(BF16), wd_ref[cols, :], preferred_element_type=F32)
        acc = part if acc is None else acc + part
    y = x + acc
    o_ref[...] = _rms(y, fg_ref[...]) if final_norm else y


def _ffn(x2d, g, w_up, cw, cb, w_down, fg, *, seq, final_norm):
    t, d = x2d.shape
    ff = w_down.shape[0]
    rb = FFN_RB
    blocks_per_seq = seq // rb
    assert ff % LANES == 0 and rb >= CONV_HALO and seq % rb == 0
    return pl.pallas_call(
        functools.partial(_ffn_kernel, blocks_per_seq=blocks_per_seq, final_norm=final_norm),
        grid=(t // rb,),
        in_specs=[pl.BlockSpec((rb, d), lambda i: (i, 0)),
                  _resident((1, d)), _resident(w_up.shape), _resident((CONV_W, ff)), _resident((1, ff)),
                  _resident(w_down.shape), _resident((1, d))],
        out_specs=pl.BlockSpec((rb, d), lambda i: (i, 0)),
        out_shape=jax.ShapeDtypeStruct((t, d), F32),
        scratch_shapes=[pltpu.VMEM((rb + CONV_HALO, ff), F32),
                        pltpu.VMEM((CONV_HALO, ff), F32)],
        compiler_params=_cparams(("arbitrary",), FFN_VMEM),
        name="ffn",
    )(x2d, g.reshape(1, d), w_up, cw, cb.reshape(1, ff), w_down, fg.reshape(1, d))


def kernel(x, mem, mix_norm_g, w_in, b_gate, lam_q1, lam_k1, lam_q2, lam_k2, diff_subln_g,
           w_branch_diff, w_branch_dil, w_out, cross_norm_g, mem_norm_g, w_cq, w_ckv, w_co,
           ffn_norm_g, w_up, conv_w, conv_b, w_down, final_norm_g):
    b, s, d = x.shape
    depth = w_in.shape[0]
    t = b * s
    assert s % DIL_CHUNK == 0 and s % DIFF_KB == 0 and DIFF_KB % DIFF_QB == 0 and d == DIFF_W
    assert t % INPROJ_RB == 0 and t % MIX_RB == 0 and s % CROSS_RB == 0
    xf = x.reshape(t, d)
    for l in range(depth):
        lam_init = 0.8 - 0.6 * math.exp(-0.3 * l)
        qkv, gates = _inproj(xf, mix_norm_g[l], w_in[l].astype(BF16), b_gate[l])
        qkv3 = qkv.reshape(b, s, QKV_W)
        o_diff = _diff_attn(qkv3, lam_q1[l], lam_k1[l], lam_q2[l], lam_k2[l], diff_subln_g[l],
                            lam_init=lam_init)
        o_dil = _dil_attn(qkv3)
        x1 = _mix_out(o_diff.reshape(t, DIFF_W), o_dil.reshape(t, DIL_OUT), gates, xf,
                      w_branch_diff[l].astype(BF16), w_branch_dil[l].astype(BF16),
                      w_out[l].astype(BF16))
        mk_t, mv = _mem_kv(mem, mem_norm_g[l], w_ckv[l].astype(BF16))
        x2 = _cross_attn(x1.reshape(b, s, d), cross_norm_g[l], w_cq[l].astype(BF16), mk_t, mv,
                         w_co[l].astype(BF16))
        xf = _ffn(x2.reshape(t, d), ffn_norm_g[l], w_up[l].astype(BF16), conv_w[l], conv_b[l],
                  w_down[l].astype(BF16), final_norm_g, seq=s, final_norm=(l == depth - 1))
    return xf.reshape(b, s, d)
```

```python
import functools
import math

import jax
import jax.numpy as jnp
from jax import lax
from jax.experimental import pallas as pl
from jax.experimental.pallas import tpu as pltpu

F32 = jnp.float32
BF16 = jnp.bfloat16

EPS = 1e-5
NEG = -1e30
LOG2E = math.log2(math.e)
LANES = 128

DIFF_HEADS = 8
DIFF_HD = 64
DIFF_W = DIFF_HEADS * 2 * DIFF_HD
DIFF_QB = 256
DIFF_KB = 1024
DIFF_LROWS = 16
DIL_GROUPS = ((128, 1), (512, 4), (2048, 16))
DIL_HEADS = 4
DIL_HD = 128
DIL_W = len(DIL_GROUPS) * DIL_HEADS * DIL_HD
DIL_OUT = DIL_HEADS * DIL_HD
DIL_BLK = 128
DIL_CHUNK = 2048
DIL_SUBSTRIDE = 4
X_HEADS = 4
CONV_W = 3
CONV_HALO = 8
FFN_CHUNK = 1536

INPROJ_RB, INPROJ_VMEM = 512, 60
DIFF_VMEM = 56
DIL_VMEM = 48
MIX_RB, MIX_VMEM = 1024, 56
MEMKV_VMEM = 32
CROSS_RB, CROSS_VMEM = 1024, 48
FFN_RB, FFN_VMEM = 512, 60

QKV_W = 3 * DIFF_W + 3 * DIL_W
INPROJ_NC = 512
COLBLK_DK = DIFF_W // LANES
COLBLK_DV = 2 * DIFF_W // LANES
COLBLK_SQ = 3 * DIFF_W // LANES
COLBLK_SK = COLBLK_SQ + DIL_W // LANES
COLBLK_SV = COLBLK_SK + DIL_W // LANES


def _alibi_slopes(n):
    return jnp.exp2(-8.0 * jnp.arange(1, n + 1, dtype=F32) / n)


def _cparams(sem, vmem_mb):
    return pltpu.CompilerParams(dimension_semantics=sem, vmem_limit_bytes=vmem_mb * 1024 * 1024)


def _rms(x, g):
    return x * lax.rsqrt(jnp.mean(x * x, axis=-1, keepdims=True) + EPS) * g


def _resident(shape):
    return pl.BlockSpec(shape, lambda *_: (0,) * len(shape), pipeline_mode=pl.Buffered(1))


def _inproj_kernel(x_ref, g_ref, w_ref, b_ref, qkv_ref, gate_ref):
    h = _rms(x_ref[...], g_ref[...]).astype(BF16)
    nq, ng, nc = qkv_ref.shape[1], gate_ref.shape[1], INPROJ_NC
    for c0 in range(0, nq, nc):
        acc = jnp.dot(h, w_ref[:, pl.ds(c0, nc)], preferred_element_type=F32)
        if c0 < DIFF_W:
            acc = acc * (LOG2E * DIFF_HD ** -0.5)
        elif 3 * DIFF_W <= c0 < 3 * DIFF_W + DIL_W:
            acc = acc * (LOG2E * DIL_HD ** -0.5)
        qkv_ref[:, pl.ds(c0, nc)] = acc.astype(qkv_ref.dtype)
    for c0 in range(0, ng, nc):
        acc = jnp.dot(h, w_ref[:, pl.ds(nq + c0, nc)], preferred_element_type=F32)
        gate_ref[:, pl.ds(c0, nc)] = jax.nn.sigmoid(acc + b_ref[:, pl.ds(c0, nc)])


def _inproj(x2d, g, w, b_gate):
    t, d = x2d.shape
    rb = INPROJ_RB
    ng = b_gate.shape[0]
    nq = w.shape[1] - ng
    assert DIFF_W % INPROJ_NC == 0 and DIL_W % INPROJ_NC == 0 and ng % INPROJ_NC == 0 and nq == QKV_W
    return pl.pallas_call(
        _inproj_kernel,
        grid=(t // rb,),
        in_specs=[pl.BlockSpec((rb, d), lambda i: (i, 0)),
                  _resident((1, d)), _resident(w.shape), _resident((1, ng))],
        out_specs=[pl.BlockSpec((rb, nq), lambda i: (i, 0)),
                   pl.BlockSpec((rb, ng), lambda i: (i, 0))],
        out_shape=[jax.ShapeDtypeStruct((t, nq), BF16), jax.ShapeDtypeStruct((t, ng), F32)],
        compiler_params=_cparams(("parallel",), INPROJ_VMEM),
        name="inproj",
    )(x2d, g.reshape(1, d), w, b_gate.reshape(1, ng))


def _diff_kernel(slopes_ref, q_ref, k_ref, v_ref, lq1_ref, lk1_ref, lq2_ref, lk2_ref, subg_ref,
                 o_ref, kaug_ref, vt_ref, acc_ref, ta_ref, tb_ref, *, seq, lam_init):
    qb, kb = DIFF_QB, DIFF_KB
    per = kb // qb
    slope = slopes_ref[pl.program_id(1)] * LOG2E

    lane_k = lax.broadcasted_iota(jnp.int32, (kb, LANES), 1)
    val = slope * lax.broadcasted_iota(jnp.int32, (kb, LANES), 0).astype(F32)
    hi = val.astype(BF16).astype(F32)
    mid = (val - hi).astype(BF16).astype(F32)
    lo = val - hi - mid
    zero_k = jnp.zeros((kb, LANES), F32)

    def bias_cols(first):
        return jnp.where(lane_k == first, hi,
                         jnp.where(lane_k == first + 1, mid, jnp.where(lane_k == first + 2, lo, zero_k)))

    bias1, bias2 = bias_cols(DIFF_HD), bias_cols(0)
    ones_row = jnp.where(lax.broadcasted_iota(jnp.int32, (DIFF_LROWS, kb), 0) == 0, 1.0, 0.0).astype(BF16)

    def build(jb, carry):
        r0 = pl.multiple_of(jb * kb, kb)
        kk = k_ref[pl.ds(r0, kb), :].astype(F32)
        k1 = jnp.where(lane_k < DIFF_HD, kk, bias1)
        k2 = jnp.where(lane_k >= DIFF_HD, kk, bias2)
        kaug_ref[0, jb] = k1.astype(BF16)
        kaug_ref[1, jb] = k2.astype(BF16)
        vt_ref[jb, pl.ds(0, LANES), :] = v_ref[pl.ds(r0, kb), :].astype(F32).T.astype(BF16)
        vt_ref[jb, pl.ds(LANES, DIFF_LROWS), :] = ones_row
        return carry

    lax.fori_loop(0, seq // kb, build, 0)

    lam = (jnp.exp(jnp.sum(lq1_ref[...] * lk1_ref[...], axis=-1, keepdims=True))
           - jnp.exp(jnp.sum(lq2_ref[...] * lk2_ref[...], axis=-1, keepdims=True)) + lam_init)
    lane_q = lax.broadcasted_iota(jnp.int32, (qb, LANES), 1)
    one_q = jnp.ones((qb, LANES), F32)
    zero_q = jnp.zeros((qb, LANES), F32)
    qpos = lax.broadcasted_iota(jnp.int32, (1, qb), 1).astype(F32)
    causal = (lax.broadcasted_iota(jnp.int32, (qb, qb), 0)
              <= lax.broadcasted_iota(jnp.int32, (qb, qb), 1))
    bufs = (ta_ref, tb_ref)

    def scores(q_t, jb, nk, t_ref, diag):
        mx = []
        for c in range(2):
            t = jnp.dot(kaug_ref[c, jb, pl.ds(0, nk), :], q_t[c], preferred_element_type=F32)
            if diag:
                tri = jnp.where(causal, t[nk - qb:, :], NEG)
                t_ref[c, pl.ds(nk - qb, qb), :] = tri
                m = jnp.max(tri, axis=0, keepdims=True)
                if nk > qb:
                    t_ref[c, pl.ds(0, nk - qb), :] = t[:nk - qb, :]
                    m = jnp.maximum(m, jnp.max(t[:nk - qb, :], axis=0, keepdims=True))
            else:
                t_ref[c] = t
                m = jnp.max(t, axis=0, keepdims=True)
            mx.append(m)
        return tuple(mx)

    def accumulate(strip, jb, nk, dist0, t_ref, mx, state):
        u = slope * (jnp.asarray(dist0, F32) + qpos)
        new_state = []
        for c in range(2):
            m_new = mx[c] - u if state is None else jnp.maximum(state[c], mx[c] - u)
            p = jnp.exp2(t_ref[c, pl.ds(0, nk), :] - (m_new + u))
            pv = jnp.dot(vt_ref[jb, :, pl.ds(0, nk)], p.astype(BF16), preferred_element_type=F32)
            if state is None:
                acc_ref[strip, c] = pv
            else:
                acc_ref[strip, c] = jnp.exp2(state[c] - m_new) * acc_ref[strip, c] + pv
            new_state.append(m_new)
        return tuple(new_state)

    def finalize(sb_done):
        for s in range(per):
            l1 = acc_ref[s, 0, pl.ds(LANES, 1), :]
            l2 = acc_ref[s, 1, pl.ds(LANES, 1), :]
            o = (acc_ref[s, 0, pl.ds(0, LANES), :] * (1.0 / l1)
                 - lam * (acc_ref[s, 1, pl.ds(0, LANES), :] * (1.0 / l2)))
            y = o * lax.rsqrt(jnp.mean(o * o, axis=0, keepdims=True) + EPS)
            i0 = pl.multiple_of(sb_done * kb + s * qb, qb)
            o_ref[pl.ds(i0, qb), :] = (y.T * subg_ref[...] * (1.0 - lam_init)).astype(o_ref.dtype)

    def super_block(sb, carry):
        finalize(jnp.maximum(sb - 1, 0))
        q_t = []
        for s in range(per):
            i0 = pl.multiple_of(sb * kb + s * qb, qb)
            qq = q_ref[pl.ds(i0, qb), :].astype(F32)
            q1 = jnp.where(lane_q < DIFF_HD, qq, jnp.where(lane_q < DIFF_HD + 3, one_q, zero_q))
            q2 = jnp.where(lane_q >= DIFF_HD, qq, jnp.where(lane_q < 3, one_q, zero_q))
            q_t.append((q1.T.astype(BF16), q2.T.astype(BF16)))

        state = []
        mx = scores(q_t[0], sb, qb, bufs[0], True)
        for s in range(per):
            if s + 1 < per:
                mx_next = scores(q_t[s + 1], sb, (s + 2) * qb, bufs[(s + 1) % 2], True)
            else:
                mx_next = scores(q_t[0], jnp.maximum(sb - 1, 0), kb, bufs[per % 2], False)
            state.append(accumulate(s, sb, (s + 1) * qb, s * qb, bufs[s % 2], mx, None))
            mx = mx_next

        def full_tile(n, mx, state):
            jb = sb - 1 - n
            dist = (n + 1) * kb
            state = list(state)
            for s in range(per):
                if s + 1 < per:
                    mx_next = scores(q_t[s + 1], jb, kb, bufs[(per + s + 1) % 2], False)
                else:
                    mx_next = scores(q_t[0], jnp.maximum(jb - 1, 0), kb, bufs[per % 2], False)
                state[s] = accumulate(s, jb, kb, dist + s * qb, bufs[(per + s) % 2], mx, state[s])
                mx = mx_next
            return mx, tuple(state)

        odd = sb % 2
        mx, state = lax.cond(odd == 1, lambda m, st: full_tile(0, m, st), lambda m, st: (m, st),
                             mx, tuple(state))

        def tile_pair(i, carry):
            mx, state = full_tile(odd + 2 * i, *carry)
            return full_tile(odd + 2 * i + 1, mx, state)

        lax.fori_loop(0, sb // 2, tile_pair, (mx, state))
        return carry

    assert per % 2 == 0
    acc_init = jnp.where(lax.broadcasted_iota(jnp.int32, (LANES + DIFF_LROWS, qb), 0) == LANES, 1.0, 0.0)
    for s in range(per):
        for c in range(2):
            acc_ref[s, c] = acc_init
    lax.fori_loop(0, seq // kb, super_block, 0)
    finalize(seq // kb - 1)


def _diff_attn(qkv, lq1, lk1, lq2, lk2, subg, *, lam_init):
    b, s, _ = qkv.shape
    qb, kb = DIFF_QB, DIFF_KB
    small = lambda w: pl.BlockSpec((1, w), lambda bi, h: (0, 0))
    return pl.pallas_call(
        functools.partial(_diff_kernel, seq=s, lam_init=lam_init),
        grid=(b, DIFF_HEADS),
        in_specs=[pl.BlockSpec(memory_space=pltpu.SMEM),
                  pl.BlockSpec((None, s, LANES), lambda bi, h: (bi, 0, h)),
                  pl.BlockSpec((None, s, LANES), lambda bi, h: (bi, 0, COLBLK_DK + h)),
                  pl.BlockSpec((None, s, LANES), lambda bi, h: (bi, 0, COLBLK_DV + h)),
                  small(DIFF_HD), small(DIFF_HD), small(DIFF_HD), small(DIFF_HD), small(2 * DIFF_HD)],
        out_specs=pl.BlockSpec((None, s, LANES), lambda bi, h: (bi, 0, h)),
        out_shape=jax.ShapeDtypeStruct((b, s, DIFF_W), BF16),
        scratch_shapes=[pltpu.VMEM((2, s // kb, kb, LANES), BF16),
                        pltpu.VMEM((s // kb, LANES + DIFF_LROWS, kb), BF16),
                        pltpu.VMEM((kb // qb, 2, LANES + DIFF_LROWS, qb), F32),
                        pltpu.VMEM((2, kb, qb), F32),
                        pltpu.VMEM((2, kb, qb), F32)],
        compiler_params=_cparams(("parallel", "parallel"), DIFF_VMEM),
        name="diff_attn",
    )(_alibi_slopes(DIFF_HEADS), qkv, qkv, qkv,
      lq1.reshape(1, -1), lk1.reshape(1, -1), lq2.reshape(1, -1), lk2.reshape(1, -1), subg.reshape(1, -1))


def _dil_kernel(slopes_ref, *refs):
    ngrp = len(DIL_GROUPS)
    in_refs = refs[:5 * ngrp]
    o_ref, qf_ref, kf_ref, vf_ref, qy_ref, ky_ref, vy_ref, og_ref, lg_ref = refs[5 * ngrp:]
    ch, blk, sub = DIL_CHUNK, DIL_BLK, DIL_SUBSTRIDE
    c = pl.program_id(1)
    slope = slopes_ref[pl.program_id(2)] * LOG2E

    qi = lax.broadcasted_iota(jnp.int32, (blk, 2 * blk), 0)
    ki = lax.broadcasted_iota(jnp.int32, (blk, 2 * blk), 1)
    step = blk + qi - ki
    bias = jnp.where((step >= 0) & (step <= blk), -slope * step.astype(F32), NEG)
    seq_start = jnp.where(ki < blk, jnp.where(c == 0, NEG, 0.0), 0.0)

    def attend(q, kk, vv, first_block):
        s = lax.dot_general(q, kk, (((1,), (1,)), ((), ())), preferred_element_type=F32) + bias
        if first_block:
            s = s + seq_start
        m = jnp.max(s, axis=-1, keepdims=True)
        p = jnp.exp2(s - m)
        den = jnp.sum(p, axis=-1, keepdims=True)
        o = jnp.dot(p.astype(BF16), vv, preferred_element_type=F32) * (1.0 / den)
        return o, jnp.broadcast_to(m + jnp.log2(den), (blk, LANES))

    def run_units(unit):
        for u in range(ch // blk):
            unit(u)

    for g, (window, dil) in enumerate(DIL_GROUPS):
        q_ref, k_ref, v_ref, kp_ref, vp_ref = in_refs[5 * g:5 * g + 5]
        span = blk * dil
        if dil == 1:
            def unit(n, g=g, q_ref=q_ref, k_ref=k_ref, v_ref=v_ref, kp_ref=kp_ref, vp_ref=vp_ref):
                q = q_ref[pl.ds(n * blk, blk), :]
                if n == 0:
                    kk = jnp.concatenate([kp_ref[...], k_ref[pl.ds(0, blk), :]], axis=0)
                    vv = jnp.concatenate([vp_ref[...], v_ref[pl.ds(0, blk), :]], axis=0)
                else:
                    kk = k_ref[pl.ds((n - 1) * blk, 2 * blk), :]
                    vv = v_ref[pl.ds((n - 1) * blk, 2 * blk), :]
                o, lse = attend(q, kk, vv, n == 0)
                og_ref[g, pl.ds(n * blk, blk), :] = o
                lg_ref[g, pl.ds(n * blk, blk), :] = lse

            run_units(unit)
            continue

        qf_ref[...] = q_ref[...].astype(F32)
        kf_ref[pl.ds(ch, ch), :] = k_ref[...].astype(F32)
        vf_ref[pl.ds(ch, ch), :] = v_ref[...].astype(F32)
        kf_ref[pl.ds(ch - span, span), :] = kp_ref[...].astype(F32)
        vf_ref[pl.ds(ch - span, span), :] = vp_ref[...].astype(F32)
        two_stage = dil > sub
        if two_stage:
            assert span == ch and dil % sub == 0
            for r0 in range(sub):
                qy_ref[r0] = qf_ref[pl.ds(r0, ch // sub, stride=sub), :]
                ky_ref[r0] = kf_ref[pl.ds(r0, 2 * ch // sub, stride=sub), :]
                vy_ref[r0] = vf_ref[pl.ds(r0, 2 * ch // sub, stride=sub), :]

        def unit(uidx, g=g, dil=dil, span=span, two_stage=two_stage):
            n_local, r = divmod(uidx, dil)
            q0 = n_local * span + r
            k0 = ch - span + q0
            if two_stage:
                r1, r0 = divmod(r, sub)
                q = qy_ref[r0, pl.ds(r1, blk, stride=dil // sub), :]
                kk = ky_ref[r0, pl.ds(r1, 2 * blk, stride=dil // sub), :]
                vv = vy_ref[r0, pl.ds(r1, 2 * blk, stride=dil // sub), :]
            else:
                q = qf_ref[pl.ds(q0, blk, stride=dil), :]
                kk = kf_ref[pl.ds(k0, 2 * blk, stride=dil), :]
                vv = vf_ref[pl.ds(k0, 2 * blk, stride=dil), :]
            o, lse = attend(q.astype(BF16), kk.astype(BF16), vv.astype(BF16), n_local == 0)
            og_ref[g, pl.ds(q0, blk, stride=dil), :] = o
            lg_ref[g, pl.ds(q0, blk, stride=dil), :] = lse

        run_units(unit)

    rows = 256
    for r0 in range(0, ch, rows):
        ls = [lg_ref[g, pl.ds(r0, rows), :] for g in range(ngrp)]
        top = functools.reduce(jnp.maximum, ls)
        ws = [jnp.exp2(x - top) for x in ls]
        num = sum(w * og_ref[g, pl.ds(r0, rows), :] for g, w in enumerate(ws))
        o_ref[pl.ds(r0, rows), :] = (num * (1.0 / sum(ws))).astype(o_ref.dtype)


def _dil_attn(qkv):
    b, s, _ = qkv.shape
    ch, blk = DIL_CHUNK, DIL_BLK
    in_specs = [pl.BlockSpec(memory_space=pltpu.SMEM)]
    args = [_alibi_slopes(DIL_HEADS)]
    for g, (window, dil) in enumerate(DIL_GROUPS):
        span = blk * dil
        per = ch // span
        col = g * DIL_HEADS

        def cur(base, col=col):
            return pl.BlockSpec((None, ch, LANES), lambda bi, c, h: (bi, c, base + col + h))

        def prev(base, col=col, span=span, per=per):
            return pl.BlockSpec((None, span, LANES),
                                lambda bi, c, h: (bi, jnp.maximum(c * per - 1, 0), base + col + h))

        in_specs += [cur(COLBLK_SQ), cur(COLBLK_SK), cur(COLBLK_SV), prev(COLBLK_SK), prev(COLBLK_SV)]
        args += [qkv] * 5
    return pl.pallas_call(
        _dil_kernel,
        grid=(b, s // ch, DIL_HEADS),
        in_specs=in_specs,
        out_specs=pl.BlockSpec((None, ch, LANES), lambda bi, c, h: (bi, c, h)),
        out_shape=jax.ShapeDtypeStruct((b, s, DIL_OUT), BF16),
        scratch_shapes=[pltpu.VMEM((ch, LANES), F32),
                        pltpu.VMEM((2 * ch, LANES), F32),
                        pltpu.VMEM((2 * ch, LANES), F32),
                        pltpu.VMEM((DIL_SUBSTRIDE, ch // DIL_SUBSTRIDE, LANES), F32),
                        pltpu.VMEM((DIL_SUBSTRIDE, 2 * ch // DIL_SUBSTRIDE, LANES), F32),
                        pltpu.VMEM((DIL_SUBSTRIDE, 2 * ch // DIL_SUBSTRIDE, LANES), F32),
                        pltpu.VMEM((len(DIL_GROUPS), ch, LANES), F32),
                        pltpu.VMEM((len(DIL_GROUPS), ch, LANES), F32)],
        compiler_params=_cparams(("parallel", "parallel", "parallel"), DIL_VMEM),
        name="dil_attn",
    )(*args)


def _mix_kernel(od_ref, ol_ref, gd_ref, gl_ref, x_ref, wbd_ref, wbl_ref, wo_ref, o_ref):
    yd = jnp.dot(od_ref[...], wbd_ref[...], preferred_element_type=F32)
    yl = jnp.dot(ol_ref[...], wbl_ref[...], preferred_element_type=F32)
    y = gd_ref[...] * yd + gl_ref[...] * yl
    o_ref[...] = x_ref[...] + jnp.dot(y.astype(BF16), wo_ref[...], preferred_element_type=F32)


def _mix_out(od, ol, gates, x2d, wbd, wbl, wo):
    t, d = x2d.shape
    rb = MIX_RB
    return pl.pallas_call(
        _mix_kernel,
        grid=(t // rb,),
        in_specs=[pl.BlockSpec((rb, DIFF_W), lambda i: (i, 0)),
                  pl.BlockSpec((rb, DIL_OUT), lambda i: (i, 0)),
                  pl.BlockSpec((rb, d), lambda i: (i, 0)),
                  pl.BlockSpec((rb, d), lambda i: (i, 1)),
                  pl.BlockSpec((rb, d), lambda i: (i, 0)),
                  _resident(wbd.shape), _resident(wbl.shape), _resident(wo.shape)],
        out_specs=pl.BlockSpec((rb, d), lambda i: (i, 0)),
        out_shape=jax.ShapeDtypeStruct((t, d), F32),
        compiler_params=_cparams(("parallel",), MIX_VMEM),
        name="mix_out",
    )(od, ol, gates, gates, x2d, wbd, wbl, wo)


def _memkv_kernel(mem_ref, g_ref, w_ref, kt_ref, v_ref):
    d = mem_ref.shape[-1]
    mn = _rms(mem_ref[...], g_ref[...]).astype(BF16)
    kv = jnp.dot(mn, w_ref[...], preferred_element_type=F32)
    kt_ref[...] = kv[:, :d].T.astype(BF16)
    v_ref[...] = kv[:, d:].astype(BF16)


def _mem_kv(mem, g, w):
    b, n, d = mem.shape
    return pl.pallas_call(
        _memkv_kernel,
        grid=(b,),
        in_specs=[pl.BlockSpec((None, n, d), lambda bi: (bi, 0, 0)),
                  pl.BlockSpec((1, d), lambda bi: (0, 0)),
                  pl.BlockSpec((d, 2 * d), lambda bi: (0, 0))],
        out_specs=[pl.BlockSpec((None, d, n), lambda bi: (bi, 0, 0)),
                   pl.BlockSpec((None, n, d), lambda bi: (bi, 0, 0))],
        out_shape=[jax.ShapeDtypeStruct((b, d, n), BF16), jax.ShapeDtypeStruct((b, n, d), BF16)],
        compiler_params=_cparams(("parallel",), MEMKV_VMEM),
        name="mem_kv",
    )(mem, g.reshape(1, d), w)


def _cross_kernel(x_ref, g_ref, wq_ref, kt_ref, v_ref, wo_ref, o_ref):
    d = x_ref.shape[-1]
    hd = d // X_HEADS
    x = x_ref[...]
    hq = _rms(x, g_ref[...]).astype(BF16)
    q = (jnp.dot(hq, wq_ref[...], preferred_element_type=F32) * (hd ** -0.5)).astype(BF16)
    outs = []
    for h in range(X_HEADS):
        sl = slice(h * hd, (h + 1) * hd)
        s = jnp.dot(q[:, sl], kt_ref[sl, :], preferred_element_type=F32)
        m = jnp.max(s, axis=-1, keepdims=True)
        p = jnp.exp(s - m)
        den = jnp.sum(p, axis=-1, keepdims=True)
        outs.append(jnp.dot(p.astype(BF16), v_ref[:, sl], preferred_element_type=F32) * (1.0 / den))
    o = jnp.concatenate(outs, axis=-1).astype(BF16)
    o_ref[...] = x + jnp.dot(o, wo_ref[...], preferred_element_type=F32)


def _cross_attn(x1, g, wq, kt, v, wo):
    b, s, d = x1.shape
    rb = CROSS_RB
    n = v.shape[1]
    return pl.pallas_call(
        _cross_kernel,
        grid=(b, s // rb),
        in_specs=[pl.BlockSpec((None, rb, d), lambda bi, i: (bi, i, 0)),
                  _resident((1, d)), _resident((d, d)),
                  pl.BlockSpec((None, d, n), lambda bi, i: (bi, 0, 0)),
                  pl.BlockSpec((None, n, d), lambda bi, i: (bi, 0, 0)),
                  _resident((d, d))],
        out_specs=pl.BlockSpec((None, rb, d), lambda bi, i: (bi, i, 0)),
        out_shape=jax.ShapeDtypeStruct((b, s, d), F32),
        compiler_params=_cparams(("parallel", "parallel"), CROSS_VMEM),
        name="cross_attn",
    )(x1, g.reshape(1, d), wq, kt, v, wo)


def _ffn_kernel(x_ref, g_ref, wu_ref, cw_ref, cb_ref, wd_ref, fg_ref, o_ref, abuf_ref, carry_ref,
                *, blocks_per_seq, final_norm):
    rb = x_ref.shape[0]
    ff = wd_ref.shape[0]
    halo = CONV_HALO
    x = x_ref[...]
    h = _rms(x, g_ref[...]).astype(BF16)
    first = (pl.program_id(0) % blocks_per_seq) == 0
    acc = None
    for c0 in range(0, ff, FFN_CHUNK):
        fc = min(FFN_CHUNK, ff - c0)
        cols = pl.ds(c0, fc)
        a = jnp.dot(h, wu_ref[:, cols], preferred_element_type=F32)
        bgate = jnp.dot(h, wu_ref[:, pl.ds(ff + c0, fc)], preferred_element_type=F32)
        prev = carry_ref[:, cols]
        abuf_ref[pl.ds(0, halo), cols] = jnp.where(first, jnp.zeros_like(prev), prev)
        abuf_ref[pl.ds(halo, rb), cols] = a
        carry_ref[:, cols] = a[rb - halo:, :]
        cw = cw_ref[:, cols]
        y = (cw[2:3, :] * a + cw[1:2, :] * abuf_ref[pl.ds(halo - 1, rb), cols]
             + cw[0:1, :] * abuf_ref[pl.ds(halo - 2, rb), cols] + cb_ref[:, cols])
        act = 0.5 * y * (1.0 + lax.erf(y * (2.0 ** -0.5)))
        part = jnp.dot((act * bgate).astype(BF16), wd_ref[cols, :], preferred_element_type=F32)
        acc = part if acc is None else acc + part
    y = x + acc
    o_ref[...] = _rms(y, fg_ref[...]) if final_norm else y


def _ffn(x2d, g, w_up, cw, cb, w_down, fg, *, seq, final_norm):
    t, d = x2d.shape
    ff = w_down.shape[0]
    rb = FFN_RB
    blocks_per_seq = seq // rb
    assert ff % LANES == 0 and rb >= CONV_HALO and seq % rb == 0
    return pl.pallas_call(
        functools.partial(_ffn_kernel, blocks_per_seq=blocks_per_seq, final_norm=final_norm),
        grid=(t // rb,),
        in_specs=[pl.BlockSpec((rb, d), lambda i: (i, 0)),
                  _resident((1, d)), _resident(w_up.shape), _resident((CONV_W, ff)), _resident((1, ff)),
                  _resident(w_down.shape), _resident((1, d))],
        out_specs=pl.BlockSpec((rb, d), lambda i: (i, 0)),
        out_shape=jax.ShapeDtypeStruct((t, d), F32),
        scratch_shapes=[pltpu.VMEM((rb + CONV_HALO, ff), F32),
                        pltpu.VMEM((CONV_HALO, ff), F32)],
        compiler_params=_cparams(("arbitrary",), FFN_VMEM),
        name="ffn",
    )(x2d, g.reshape(1, d), w_up, cw, cb.reshape(1, ff), w_down, fg.reshape(1, d))


def kernel(x, mem, mix_norm_g, w_in, b_gate, lam_q1, lam_k1, lam_q2, lam_k2, diff_subln_g,
           w_branch_diff, w_branch_dil, w_out, cross_norm_g, mem_norm_g, w_cq, w_ckv, w_co,
           ffn_norm_g, w_up, conv_w, conv_b, w_down, final_norm_g):
    b, s, d = x.shape
    depth = w_in.shape[0]
    t = b * s
    assert s % DIL_CHUNK == 0 and s % DIFF_KB == 0 and DIFF_KB % DIFF_QB == 0 and d == DIFF_W
    assert t % INPROJ_RB == 0 and t % MIX_RB == 0 and s % CROSS_RB == 0
    xf = x.reshape(t, d)
    for l in range(depth):
        lam_init = 0.8 - 0.6 * math.exp(-0.3 * l)
        qkv, gates = _inproj(xf, mix_norm_g[l], w_in[l].astype(BF16), b_gate[l])
        qkv3 = qkv.reshape(b, s, QKV_W)
        o_diff = _diff_attn(qkv3, lam_q1[l], lam_k1[l], lam_q2[l], lam_k2[l], diff_subln_g[l],
                            lam_init=lam_init)
        o_dil = _dil_attn(qkv3)
        x1 = _mix_out(o_diff.reshape(t, DIFF_W), o_dil.reshape(t, DIL_OUT), gates, xf,
                      w_branch_diff[l].astype(BF16), w_branch_dil[l].astype(BF16),
                      w_out[l].astype(BF16))
        mk_t, mv = _mem_kv(mem, mem_norm_g[l], w_ckv[l].astype(BF16))
        x2 = _cross_attn(x1.reshape(b, s, d), cross_norm_g[l], w_cq[l].astype(BF16), mk_t, mv,
                         w_co[l].astype(BF16))
        xf = _ffn(x2.reshape(t, d), ffn_norm_g[l], w_up[l].astype(BF16), conv_w[l], conv_b[l],
                  w_down[l].astype(BF16), final_norm_g, seq=s, final_norm=(l == depth - 1))
    return xf.reshape(b, s, d)
```

```python
import functools
import math

import jax
import jax.numpy as jnp
from jax import lax
from jax.experimental import pallas as pl
from jax.experimental.pallas import tpu as pltpu

F32 = jnp.float32
BF16 = jnp.bfloat16

EPS = 1e-5
NEG = -1e30
LOG2E = math.log2(math.e)
LANES = 128

DIFF_HEADS = 8
DIFF_HD = 64
DIFF_W = DIFF_HEADS * 2 * DIFF_HD
DIFF_QB = 256
DIFF_KB = 1024
DIFF_LROWS = 16
DIL_GROUPS = ((128, 1), (512, 4), (2048, 16))
DIL_HEADS = 4
DIL_HD = 128
DIL_W = len(DIL_GROUPS) * DIL_HEADS * DIL_HD
DIL_OUT = DIL_HEADS * DIL_HD
DIL_BLK = 128
DIL_CHUNK = 2048
DIL_SUBSTRIDE = 4
X_HEADS = 4
CONV_W = 3
CONV_HALO = 8
FFN_CHUNK = 1536

INPROJ_RB, INPROJ_VMEM = 512, 60
DIFF_VMEM = 56
DIL_VMEM = 48
MIX_RB, MIX_VMEM = 1024, 56
MEMKV_VMEM = 32
CROSS_RB, CROSS_VMEM = 1024, 48
FFN_RB, FFN_VMEM = 512, 60

QKV_W = 3 * DIFF_W + 3 * DIL_W
INPROJ_NC = 512
COLBLK_DK = DIFF_W // LANES
COLBLK_DV = 2 * DIFF_W // LANES
COLBLK_SQ = 3 * DIFF_W // LANES
COLBLK_SK = COLBLK_SQ + DIL_W // LANES
COLBLK_SV = COLBLK_SK + DIL_W // LANES


def _alibi_slopes(n):
    return jnp.exp2(-8.0 * jnp.arange(1, n + 1, dtype=F32) / n)


def _cparams(sem, vmem_mb):
    return pltpu.CompilerParams(dimension_semantics=sem, vmem_limit_bytes=vmem_mb * 1024 * 1024)


def _rms(x, g):
    return x * lax.rsqrt(jnp.mean(x * x, axis=-1, keepdims=True) + EPS) * g


def _resident(shape):
    return pl.BlockSpec(shape, lambda *_: (0,) * len(shape), pipeline_mode=pl.Buffered(1))


def _inproj_kernel(x_ref, g_ref, w_ref, b_ref, qkv_ref, gate_ref):
    h = _rms(x_ref[...], g_ref[...]).astype(BF16)
    nq, ng, nc = qkv_ref.shape[1], gate_ref.shape[1], INPROJ_NC
    for c0 in range(0, nq, nc):
        acc = jnp.dot(h, w_ref[:, pl.ds(c0, nc)], preferred_element_type=F32)
        if c0 < DIFF_W:
            acc = acc * (LOG2E * DIFF_HD ** -0.5)
        elif 3 * DIFF_W <= c0 < 3 * DIFF_W + DIL_W:
            acc = acc * (LOG2E * DIL_HD ** -0.5)
        qkv_ref[:, pl.ds(c0, nc)] = acc.astype(qkv_ref.dtype)
    for c0 in range(0, ng, nc):
        acc = jnp.dot(h, w_ref[:, pl.ds(nq + c0, nc)], preferred_element_type=F32)
        gate_ref[:, pl.ds(c0, nc)] = jax.nn.sigmoid(acc + b_ref[:, pl.ds(c0, nc)])


def _inproj(x2d, g, w, b_gate):
    t, d = x2d.shape
    rb = INPROJ_RB
    ng = b_gate.shape[0]
    nq = w.shape[1] - ng
    assert DIFF_W % INPROJ_NC == 0 and DIL_W % INPROJ_NC == 0 and ng % INPROJ_NC == 0 and nq == QKV_W
    return pl.pallas_call(
        _inproj_kernel,
        grid=(t // rb,),
        in_specs=[pl.BlockSpec((rb, d), lambda i: (i, 0)),
                  _resident((1, d)), _resident(w.shape), _resident((1, ng))],
        out_specs=[pl.BlockSpec((rb, nq), lambda i: (i, 0)),
                   pl.BlockSpec((rb, ng), lambda i: (i, 0))],
        out_shape=[jax.ShapeDtypeStruct((t, nq), BF16), jax.ShapeDtypeStruct((t, ng), F32)],
        compiler_params=_cparams(("parallel",), INPROJ_VMEM),
        name="inproj",
    )(x2d, g.reshape(1, d), w, b_gate.reshape(1, ng))


def _diff_kernel(slopes_ref, q_ref, k_ref, v_ref, lq1_ref, lk1_ref, lq2_ref, lk2_ref, subg_ref,
                 o_ref, kaug_ref, vt_ref, acc_ref, ta_ref, tb_ref, *, seq, lam_init):
    qb, kb = DIFF_QB, DIFF_KB
    per = kb // qb
    slope = slopes_ref[pl.program_id(1)] * LOG2E

    lane_k = lax.broadcasted_iota(jnp.int32, (kb, LANES), 1)
    val = slope * lax.broadcasted_iota(jnp.int32, (kb, LANES), 0).astype(F32)
    hi = val.astype(BF16).astype(F32)
    mid = (val - hi).astype(BF16).astype(F32)
    lo = val - hi - mid
    zero_k = jnp.zeros((kb, LANES), F32)

    def bias_cols(first):
        return jnp.where(lane_k == first, hi,
                         jnp.where(lane_k == first + 1, mid, jnp.where(lane_k == first + 2, lo, zero_k)))

    bias1, bias2 = bias_cols(DIFF_HD), bias_cols(0)
    ones_row = jnp.where(lax.broadcasted_iota(jnp.int32, (DIFF_LROWS, kb), 0) == 0, 1.0, 0.0).astype(BF16)

    def build(jb, carry):
        r0 = pl.multiple_of(jb * kb, kb)
        kk = k_ref[pl.ds(r0, kb), :].astype(F32)
        k1 = jnp.where(lane_k < DIFF_HD, kk, bias1)
        k2 = jnp.where(lane_k >= DIFF_HD, kk, bias2)
        kaug_ref[0, jb] = k1.astype(BF16)
        kaug_ref[1, jb] = k2.astype(BF16)
        vt_ref[jb, pl.ds(0, LANES), :] = v_ref[pl.ds(r0, kb), :].astype(F32).T.astype(BF16)
        vt_ref[jb, pl.ds(LANES, DIFF_LROWS), :] = ones_row
        return carry

    lax.fori_loop(0, seq // kb, build, 0)

    lam = (jnp.exp(jnp.sum(lq1_ref[...] * lk1_ref[...], axis=-1, keepdims=True))
           - jnp.exp(jnp.sum(lq2_ref[...] * lk2_ref[...], axis=-1, keepdims=True)) + lam_init)
    lane_q = lax.broadcasted_iota(jnp.int32, (qb, LANES), 1)
    one_q = jnp.ones((qb, LANES), F32)
    zero_q = jnp.zeros((qb, LANES), F32)
    qpos = lax.broadcasted_iota(jnp.int32, (1, qb), 1).astype(F32)
    causal = (lax.broadcasted_iota(jnp.int32, (qb, qb), 0)
              <= lax.broadcasted_iota(jnp.int32, (qb, qb), 1))
    bufs = (ta_ref, tb_ref)

    def scores(q_t, jb, nk, t_ref, diag):
        mx = []
        for c in range(2):
            t = jnp.dot(kaug_ref[c, jb, pl.ds(0, nk), :], q_t[c], preferred_element_type=F32)
            if diag:
                tri = jnp.where(causal, t[nk - qb:, :], NEG)
                t_ref[c, pl.ds(nk - qb, qb), :] = tri
                m = jnp.max(tri, axis=0, keepdims=True)
                if nk > qb:
                    t_ref[c, pl.ds(0, nk - qb), :] = t[:nk - qb, :]
                    m = jnp.maximum(m, jnp.max(t[:nk - qb, :], axis=0, keepdims=True))
            else:
                t_ref[c] = t
                m = jnp.max(t, axis=0, keepdims=True)
            mx.append(m)
        return tuple(mx)

    def accumulate(strip, jb, nk, dist0, t_ref, mx, state):
        u = slope * (jnp.asarray(dist0, F32) + qpos)
        new_state = []
        for c in range(2):
            m_new = mx[c] - u if state is None else jnp.maximum(state[c], mx[c] - u)
            p = jnp.exp2(t_ref[c, pl.ds(0, nk), :] - (m_new + u))
            pv = jnp.dot(vt_ref[jb, :, pl.ds(0, nk)], p.astype(BF16), preferred_element_type=F32)
            if state is None:
                acc_ref[strip, c] = pv
            else:
                acc_ref[strip, c] = jnp.exp2(state[c] - m_new) * acc_ref[strip, c] + pv
            new_state.append(m_new)
        return tuple(new_state)

    def finalize(sb_done):
        for s in range(per):
            l1 = acc_ref[s, 0, pl.ds(LANES, 1), :]
            l2 = acc_ref[s, 1, pl.ds(LANES, 1), :]
            o = (acc_ref[s, 0, pl.ds(0, LANES), :] * (1.0 / l1)
                 - lam * (acc_ref[s, 1, pl.ds(0, LANES), :] * (1.0 / l2)))
            y = o * lax.rsqrt(jnp.mean(o * o, axis=0, keepdims=True) + EPS)
            i0 = pl.multiple_of(sb_done * kb + s * qb, qb)
            o_ref[pl.ds(i0, qb), :] = (y.T * subg_ref[...] * (1.0 - lam_init)).astype(o_ref.dtype)

    def super_block(sb, carry):
        finalize(jnp.maximum(sb - 1, 0))
        q_t = []
        for s in range(per):
            i0 = pl.multiple_of(sb * kb + s * qb, qb)
            qq = q_ref[pl.ds(i0, qb), :].astype(F32)
            q1 = jnp.where(lane_q < DIFF_HD, qq, jnp.where(lane_q < DIFF_HD + 3, one_q, zero_q))
            q2 = jnp.where(lane_q >= DIFF_HD, qq, jnp.where(lane_q < 3, one_q, zero_q))
            q_t.append((q1.T.astype(BF16), q2.T.astype(BF16)))

        state = []
        mx = scores(q_t[0], sb, qb, bufs[0], True)
        for s in range(per):
            if s + 1 < per:
                mx_next = scores(q_t[s + 1], sb, (s + 2) * qb, bufs[(s + 1) % 2], True)
            else:
                mx_next = scores(q_t[0], jnp.maximum(sb - 1, 0), kb, bufs[per % 2], False)
            state.append(accumulate(s, sb, (s + 1) * qb, s * qb, bufs[s % 2], mx, None))
            mx = mx_next

        def full_tile(n, mx, state, more):
            jb = sb - 1 - n
            dist = (n + 1) * kb
            state = list(state)
            for s in range(per):
                if s + 1 < per:
                    mx_next = scores(q_t[s + 1], jb, kb, bufs[(per + s + 1) % 2], False)
                elif more:
                    mx_next = scores(q_t[0], jb - 1, kb, bufs[per % 2], False)
                else:
                    mx_next = mx
                state[s] = accumulate(s, jb, kb, dist + s * qb, bufs[(per + s) % 2], mx, state[s])
                mx = mx_next
            return mx, tuple(state)

        def keep(mx, state):
            return mx, state

        odd = sb % 2
        pairs = sb // 2
        mx, state = lax.cond(
            odd == 1,
            lambda m, st: lax.cond(sb == 1, lambda a, b: full_tile(0, a, b, False),
                                   lambda a, b: full_tile(0, a, b, True), m, st),
            keep, mx, tuple(state))

        def tile_pair(i, carry):
            mx, state = full_tile(odd + 2 * i, *carry, True)
            return full_tile(odd + 2 * i + 1, mx, state, True)

        mx, state = lax.fori_loop(0, jnp.maximum(pairs - 1, 0), tile_pair, (mx, state))

        def last_pair(mx, state):
            n = odd + 2 * (pairs - 1)
            mx, state = full_tile(n, mx, state, True)
            return full_tile(n + 1, mx, state, False)

        lax.cond(pairs >= 1, last_pair, keep, mx, state)
        return carry

    assert per % 2 == 0
    acc_init = jnp.where(lax.broadcasted_iota(jnp.int32, (LANES + DIFF_LROWS, qb), 0) == LANES, 1.0, 0.0)
    for s in range(per):
        for c in range(2):
            acc_ref[s, c] = acc_init
    lax.fori_loop(0, seq // kb, super_block, 0)
    finalize(seq // kb - 1)


def _diff_attn(qkv, lq1, lk1, lq2, lk2, subg, *, lam_init):
    b, s, _ = qkv.shape
    qb, kb = DIFF_QB, DIFF_KB
    small = lambda w: pl.BlockSpec((1, w), lambda bi, h: (0, 0))
    return pl.pallas_call(
        functools.partial(_diff_kernel, seq=s, lam_init=lam_init),
        grid=(b, DIFF_HEADS),
        in_specs=[pl.BlockSpec(memory_space=pltpu.SMEM),
                  pl.BlockSpec((None, s, LANES), lambda bi, h: (bi, 0, h)),
                  pl.BlockSpec((None, s, LANES), lambda bi, h: (bi, 0, COLBLK_DK + h)),
                  pl.BlockSpec((None, s, LANES), lambda bi, h: (bi, 0, COLBLK_DV + h)),
                  small(DIFF_HD), small(DIFF_HD), small(DIFF_HD), small(DIFF_HD), small(2 * DIFF_HD)],
        out_specs=pl.BlockSpec((None, s, LANES), lambda bi, h: (bi, 0, h)),
        out_shape=jax.ShapeDtypeStruct((b, s, DIFF_W), BF16),
        scratch_shapes=[pltpu.VMEM((2, s // kb, kb, LANES), BF16),
                        pltpu.VMEM((s // kb, LANES + DIFF_LROWS, kb), BF16),
                        pltpu.VMEM((kb // qb, 2, LANES + DIFF_LROWS, qb), F32),
                        pltpu.VMEM((2, kb, qb), F32),
                        pltpu.VMEM((2, kb, qb), F32)],
        compiler_params=_cparams(("parallel", "parallel"), DIFF_VMEM),
        name="diff_attn",
    )(_alibi_slopes(DIFF_HEADS), qkv, qkv, qkv,
      lq1.reshape(1, -1), lk1.reshape(1, -1), lq2.reshape(1, -1), lk2.reshape(1, -1), subg.reshape(1, -1))


def _dil_kernel(slopes_ref, *refs):
    ngrp = len(DIL_GROUPS)
    in_refs = refs[:5 * ngrp]
    o_ref, qf_ref, kf_ref, vf_ref, qy_ref, ky_ref, vy_ref, og_ref, lg_ref = refs[5 * ngrp:]
    ch, blk, sub = DIL_CHUNK, DIL_BLK, DIL_SUBSTRIDE
    c = pl.program_id(1)
    slope = slopes_ref[pl.program_id(2)] * LOG2E

    qi = lax.broadcasted_iota(jnp.int32, (blk, 2 * blk), 0)
    ki = lax.broadcasted_iota(jnp.int32, (blk, 2 * blk), 1)
    step = blk + qi - ki
    bias = jnp.where((step >= 0) & (step <= blk), -slope * step.astype(F32), NEG)
    seq_start = jnp.where(ki < blk, jnp.where(c == 0, NEG, 0.0), 0.0)

    def attend(q, kk, vv, first_block):
        s = lax.dot_general(q, kk, (((1,), (1,)), ((), ())), preferred_element_type=F32) + bias
        if first_block:
            s = s + seq_start
        m = jnp.max(s, axis=-1, keepdims=True)
        p = jnp.exp2(s - m)
        den = jnp.sum(p, axis=-1, keepdims=True)
        o = jnp.dot(p.astype(BF16), vv, preferred_element_type=F32) * (1.0 / den)
        return o, jnp.broadcast_to(m + jnp.log2(den), (blk, LANES))

    def run_units(unit):
        for u in range(ch // blk):
            unit(u)

    for g, (window, dil) in enumerate(DIL_GROUPS):
        q_ref, k_ref, v_ref, kp_ref, vp_ref = in_refs[5 * g:5 * g + 5]
        span = blk * dil
        if dil == 1:
            def unit(n, g=g, q_ref=q_ref, k_ref=k_ref, v_ref=v_ref, kp_ref=kp_ref, vp_ref=vp_ref):
                q = q_ref[pl.ds(n * blk, blk), :]
                if n == 0:
                    kk = jnp.concatenate([kp_ref[...], k_ref[pl.ds(0, blk), :]], axis=0)
                    vv = jnp.concatenate([vp_ref[...], v_ref[pl.ds(0, blk), :]], axis=0)
                else:
                    kk = k_ref[pl.ds((n - 1) * blk, 2 * blk), :]
                    vv = v_ref[pl.ds((n - 1) * blk, 2 * blk), :]
                o, lse = attend(q, kk, vv, n == 0)
                og_ref[g, pl.ds(n * blk, blk), :] = o
                lg_ref[g, pl.ds(n * blk, blk), :] = lse

            run_units(unit)
            continue

        qf_ref[...] = q_ref[...].astype(F32)
        kf_ref[pl.ds(ch, ch), :] = k_ref[...].astype(F32)
        vf_ref[pl.ds(ch, ch), :] = v_ref[...].astype(F32)
        kf_ref[pl.ds(ch - span, span), :] = kp_ref[...].astype(F32)
        vf_ref[pl.ds(ch - span, span), :] = vp_ref[...].astype(F32)
        two_stage = dil > sub
        if two_stage:
            assert span == ch and dil % sub == 0
            for r0 in range(sub):
                qy_ref[r0] = qf_ref[pl.ds(r0, ch // sub, stride=sub), :]
                ky_ref[r0] = kf_ref[pl.ds(r0, 2 * ch // sub, stride=sub), :]
                vy_ref[r0] = vf_ref[pl.ds(r0, 2 * ch // sub, stride=sub), :]

        def unit(uidx, g=g, dil=dil, span=span, two_stage=two_stage):
            n_local, r = divmod(uidx, dil)
            q0 = n_local * span + r
            k0 = ch - span + q0
            if two_stage:
                r1, r0 = divmod(r, sub)
                q = qy_ref[r0, pl.ds(r1, blk, stride=dil // sub), :]
                kk = ky_ref[r0, pl.ds(r1, 2 * blk, stride=dil // sub), :]
                vv = vy_ref[r0, pl.ds(r1, 2 * blk, stride=dil // sub), :]
            else:
                q = qf_ref[pl.ds(q0, blk, stride=dil), :]
                kk = kf_ref[pl.ds(k0, 2 * blk, stride=dil), :]
                vv = vf_ref[pl.ds(k0, 2 * blk, stride=dil), :]
            o, lse = attend(q.astype(BF16), kk.astype(BF16), vv.astype(BF16), n_local == 0)
            og_ref[g, pl.ds(q0, blk, stride=dil), :] = o
            lg_ref[g, pl.ds(q0, blk, stride=dil), :] = lse

        run_units(unit)

    rows = 256
    for r0 in range(0, ch, rows):
        ls = [lg_ref[g, pl.ds(r0, rows), :] for g in range(ngrp)]
        top = functools.reduce(jnp.maximum, ls)
        ws = [jnp.exp2(x - top) for x in ls]
        num = sum(w * og_ref[g, pl.ds(r0, rows), :] for g, w in enumerate(ws))
        o_ref[pl.ds(r0, rows), :] = (num * (1.0 / sum(ws))).astype(o_ref.dtype)


def _dil_attn(qkv):
    b, s, _ = qkv.shape
    ch, blk = DIL_CHUNK, DIL_BLK
    in_specs = [pl.BlockSpec(memory_space=pltpu.SMEM)]
    args = [_alibi_slopes(DIL_HEADS)]
    for g, (window, dil) in enumerate(DIL_GROUPS):
        span = blk * dil
        per = ch // span
        col = g * DIL_HEADS

        def cur(base, col=col):
            return pl.BlockSpec((None, ch, LANES), lambda bi, c, h: (bi, c, base + col + h))

        def prev(base, col=col, span=span, per=per):
            return pl.BlockSpec((None, span, LANES),
                                lambda bi, c, h: (bi, jnp.maximum(c * per - 1, 0), base + col + h))

        in_specs += [cur(COLBLK_SQ), cur(COLBLK_SK), cur(COLBLK_SV), prev(COLBLK_SK), prev(COLBLK_SV)]
        args += [qkv] * 5
    return pl.pallas_call(
        _dil_kernel,
        grid=(b, s // ch, DIL_HEADS),
        in_specs=in_specs,
        out_specs=pl.BlockSpec((None, ch, LANES), lambda bi, c, h: (bi, c, h)),
        out_shape=jax.ShapeDtypeStruct((b, s, DIL_OUT), BF16),
        scratch_shapes=[pltpu.VMEM((ch, LANES), F32),
                        pltpu.VMEM((2 * ch, LANES), F32),
                        pltpu.VMEM((2 * ch, LANES), F32),
                        pltpu.VMEM((DIL_SUBSTRIDE, ch // DIL_SUBSTRIDE, LANES), F32),
                        pltpu.VMEM((DIL_SUBSTRIDE, 2 * ch // DIL_SUBSTRIDE, LANES), F32),
                        pltpu.VMEM((DIL_SUBSTRIDE, 2 * ch // DIL_SUBSTRIDE, LANES), F32),
                        pltpu.VMEM((len(DIL_GROUPS), ch, LANES), F32),
                        pltpu.VMEM((len(DIL_GROUPS), ch, LANES), F32)],
        compiler_params=_cparams(("parallel", "parallel", "parallel"), DIL_VMEM),
        name="dil_attn",
    )(*args)


def _mix_kernel(od_ref, ol_ref, gd_ref, gl_ref, x_ref, wbd_ref, wbl_ref, wo_ref, o_ref):
    yd = jnp.dot(od_ref[...], wbd_ref[...], preferred_element_type=F32)
    yl = jnp.dot(ol_ref[...], wbl_ref[...], preferred_element_type=F32)
    y = gd_ref[...] * yd + gl_ref[...] * yl
    o_ref[...] = x_ref[...] + jnp.dot(y.astype(BF16), wo_ref[...], preferred_element_type=F32)


def _mix_out(od, ol, gates, x2d, wbd, wbl, wo):
    t, d = x2d.shape
    rb = MIX_RB
    return pl.pallas_call(
        _mix_kernel,
        grid=(t // rb,),
        in_specs=[pl.BlockSpec((rb, DIFF_W), lambda i: (i, 0)),
                  pl.BlockSpec((rb, DIL_OUT), lambda i: (i, 0)),
                  pl.BlockSpec((rb, d), lambda i: (i, 0)),
                  pl.BlockSpec((rb, d), lambda i: (i, 1)),
                  pl.BlockSpec((rb, d), lambda i: (i, 0)),
                  _resident(wbd.shape), _resident(wbl.shape), _resident(wo.shape)],
        out_specs=pl.BlockSpec((rb, d), lambda i: (i, 0)),
        out_shape=jax.ShapeDtypeStruct((t, d), F32),
        compiler_params=_cparams(("parallel",), MIX_VMEM),
        name="mix_out",
    )(od, ol, gates, gates, x2d, wbd, wbl, wo)


def _memkv_kernel(mem_ref, g_ref, w_ref, kt_ref, v_ref):
    d = mem_ref.shape[-1]
    mn = _rms(mem_ref[...], g_ref[...]).astype(BF16)
    kv = jnp.dot(mn, w_ref[...], preferred_element_type=F32)
    kt_ref[...] = kv[:, :d].T.astype(BF16)
    v_ref[...] = kv[:, d:].astype(BF16)


def _mem_kv(mem, g, w):
    b, n, d = mem.shape
    return pl.pallas_call(
        _memkv_kernel,
        grid=(b,),
        in_specs=[pl.BlockSpec((None, n, d), lambda bi: (bi, 0, 0)),
                  pl.BlockSpec((1, d), lambda bi: (0, 0)),
                  pl.BlockSpec((d, 2 * d), lambda bi: (0, 0))],
        out_specs=[pl.BlockSpec((None, d, n), lambda bi: (bi, 0, 0)),
                   pl.BlockSpec((None, n, d), lambda bi: (bi, 0, 0))],
        out_shape=[jax.ShapeDtypeStruct((b, d, n), BF16), jax.ShapeDtypeStruct((b, n, d), BF16)],
        compiler_params=_cparams(("parallel",), MEMKV_VMEM),
        name="mem_kv",
    )(mem, g.reshape(1, d), w)


def _cross_kernel(x_ref, g_ref, wq_ref, kt_ref, v_ref, wo_ref, o_ref):
    d = x_ref.shape[-1]
    hd = d // X_HEADS
    x = x_ref[...]
    hq = _rms(x, g_ref[...]).astype(BF16)
    q = (jnp.dot(hq, wq_ref[...], preferred_element_type=F32) * (hd ** -0.5)).astype(BF16)
    outs = []
    for h in range(X_HEADS):
        sl = slice(h * hd, (h + 1) * hd)
        s = jnp.dot(q[:, sl], kt_ref[sl, :], preferred_element_type=F32)
        m = jnp.max(s, axis=-1, keepdims=True)
        p = jnp.exp(s - m)
        den = jnp.sum(p, axis=-1, keepdims=True)
        outs.append(jnp.dot(p.astype(BF16), v_ref[:, sl], preferred_element_type=F32) * (1.0 / den))
    o = jnp.concatenate(outs, axis=-1).astype(BF16)
    o_ref[...] = x + jnp.dot(o, wo_ref[...], preferred_element_type=F32)


def _cross_attn(x1, g, wq, kt, v, wo):
    b, s, d = x1.shape
    rb = CROSS_RB
    n = v.shape[1]
    return pl.pallas_call(
        _cross_kernel,
        grid=(b, s // rb),
        in_specs=[pl.BlockSpec((None, rb, d), lambda bi, i: (bi, i, 0)),
                  _resident((1, d)), _resident((d, d)),
                  pl.BlockSpec((None, d, n), lambda bi, i: (bi, 0, 0)),
                  pl.BlockSpec((None, n, d), lambda bi, i: (bi, 0, 0)),
                  _resident((d, d))],
        out_specs=pl.BlockSpec((None, rb, d), lambda bi, i: (bi, i, 0)),
        out_shape=jax.ShapeDtypeStruct((b, s, d), F32),
        compiler_params=_cparams(("parallel", "parallel"), CROSS_VMEM),
        name="cross_attn",
    )(x1, g.reshape(1, d), wq, kt, v, wo)


def _ffn_kernel(x_ref, g_ref, wu_ref, cw_ref, cb_ref, wd_ref, fg_ref, o_ref, abuf_ref, carry_ref,
                *, blocks_per_seq, final_norm):
    rb = x_ref.shape[0]
    ff = wd_ref.shape[0]
    halo = CONV_HALO
    x = x_ref[...]
    h = _rms(x, g_ref[...]).astype(BF16)
    first = (pl.program_id(0) % blocks_per_seq) == 0
    acc = None
    for c0 in range(0, ff, FFN_CHUNK):
        fc = min(FFN_CHUNK, ff - c0)
        cols = pl.ds(c0, fc)
        a = jnp.dot(h, wu_ref[:, cols], preferred_element_type=F32)
        bgate = jnp.dot(h, wu_ref[:, pl.ds(ff + c0, fc)], preferred_element_type=F32)
        prev = carry_ref[:, cols]
        abuf_ref[pl.ds(0, halo), cols] = jnp.where(first, jnp.zeros_like(prev), prev)
        abuf_ref[pl.ds(halo, rb), cols] = a
        carry_ref[:, cols] = a[rb - halo:, :]
        cw = cw_ref[:, cols]
        y = (cw[2:3, :] * a + cw[1:2, :] * abuf_ref[pl.ds(halo - 1, rb), cols]
             + cw[0:1, :] * abuf_ref[pl.ds(halo - 2, rb), cols] + cb_ref[:, cols])
        act = 0.5 * y * (1.0 + lax.erf(y * (2.0 ** -0.5)))
        part = jnp.dot((act * bgate).astype(BF16), wd_ref[cols, :], preferred_element_type=F32)
        acc = part if acc is None else acc + part
    y = x + acc
    o_ref[...] = _rms(y, fg_ref[...]) if final_norm else y


def _ffn(x2d, g, w_up, cw, cb, w_down, fg, *, seq, final_norm):
    t, d = x2d.shape
    ff = w_down.shape[0]
    rb = FFN_RB
    blocks_per_seq = seq // rb
    assert ff % LANES == 0 and rb >= CONV_HALO and seq % rb == 0
    return pl.pallas_call(
        functools.partial(_ffn_kernel, blocks_per_seq=blocks_per_seq, final_norm=final_norm),
        grid=(t // rb,),
        in_specs=[pl.BlockSpec((rb, d), lambda i: (i, 0)),
                  _resident((1, d)), _resident(w_up.shape), _resident((CONV_W, ff)), _resident((1, ff)),
                  _resident(w_down.shape), _resident((1, d))],
        out_specs=pl.BlockSpec((rb, d), lambda i: (i, 0)),
        out_shape=jax.ShapeDtypeStruct((t, d), F32),
        scratch_shapes=[pltpu.VMEM((rb + CONV_HALO, ff), F32),
                        pltpu.VMEM((CONV_HALO, ff), F32)],
        compiler_params=_cparams(("arbitrary",), FFN_VMEM),
        name="ffn",
    )(x2d, g.reshape(1, d), w_up, cw, cb.reshape(1, ff), w_down, fg.reshape(1, d))


def kernel(x, mem, mix_norm_g, w_in, b_gate, lam_q1, lam_k1, lam_q2, lam_k2, diff_subln_g,
           w_branch_diff, w_branch_dil, w_out, cross_norm_g, mem_norm_g, w_cq, w_ckv, w_co,
           ffn_norm_g, w_up, conv_w, conv_b, w_down, final_norm_g):
    b, s, d = x.shape
    depth = w_in.shape[0]
    t = b * s
    assert s % DIL_CHUNK == 0 and s % DIFF_KB == 0 and DIFF_KB % DIFF_QB == 0 and d == DIFF_W
    assert t % INPROJ_RB == 0 and t % MIX_RB == 0 and s % CROSS_RB == 0
    xf = x.reshape(t, d)
    for l in range(depth):
        lam_init = 0.8 - 0.6 * math.exp(-0.3 * l)
        qkv, gates = _inproj(xf, mix_norm_g[l], w_in[l].astype(BF16), b_gate[l])
        qkv3 = qkv.reshape(b, s, QKV_W)
        o_diff = _diff_attn(qkv3, lam_q1[l], lam_k1[l], lam_q2[l], lam_k2[l], diff_subln_g[l],
                            lam_init=lam_init)
        o_dil = _dil_attn(qkv3)
        x1 = _mix_out(o_diff.reshape(t, DIFF_W), o_dil.reshape(t, DIL_OUT), gates, xf,
                      w_branch_diff[l].astype(BF16), w_branch_dil[l].astype(BF16),
                      w_out[l].astype(BF16))
        mk_t, mv = _mem_kv(mem, mem_norm_g[l], w_ckv[l].astype(BF16))
        x2 = _cross_attn(x1.reshape(b, s, d), cross_norm_g[l], w_cq[l].astype(BF16), mk_t, mv,
                         w_co[l].astype(BF16))
        xf = _ffn(x2.reshape(t, d), ffn_norm_g[l], w_up[l].astype(BF16), conv_w[l], conv_b[l],
                  w_down[l].astype(BF16), final_norm_g, seq=s, final_norm=(l == depth - 1))
    return xf.reshape(b, s, d)
```

```python
import functools
import math

import jax
import jax.numpy as jnp
from jax import lax
from jax.experimental import pallas as pl
from jax.experimental.pallas import tpu as pltpu

F32 = jnp.float32
BF16 = jnp.bfloat16

EPS = 1e-5
NEG = -1e30
LOG2E = math.log2(math.e)
LANES = 128

DIFF_HEADS = 8
DIFF_HD = 64
DIFF_W = DIFF_HEADS * 2 * DIFF_HD
DIFF_QB = 256
DIFF_KB = 1024
DIFF_LROWS = 16
DIL_GROUPS = ((128, 1), (512, 4), (2048, 16))
DIL_HEADS = 4
DIL_HD = 128
DIL_W = len(DIL_GROUPS) * DIL_HEADS * DIL_HD
DIL_OUT = DIL_HEADS * DIL_HD
DIL_BLK = 128
DIL_CHUNK = 2048
DIL_SUBSTRIDE = 4
X_HEADS = 4
CONV_W = 3
CONV_HALO = 8
FFN_CHUNK = 1536

INPROJ_RB, INPROJ_VMEM = 512, 60
DIFF_VMEM = 56
DIL_VMEM = 48
MIX_RB, MIX_VMEM = 1024, 56
MEMKV_VMEM = 32
CROSS_RB, CROSS_VMEM = 1024, 48
FFN_RB, FFN_VMEM = 512, 60

QKV_W = 3 * DIFF_W + 3 * DIL_W
INPROJ_NC = 512
COLBLK_DK = DIFF_W // LANES
COLBLK_DV = 2 * DIFF_W // LANES
COLBLK_SQ = 3 * DIFF_W // LANES
COLBLK_SK = COLBLK_SQ + DIL_W // LANES
COLBLK_SV = COLBLK_SK + DIL_W // LANES


def _alibi_slopes(n):
    return jnp.exp2(-8.0 * jnp.arange(1, n + 1, dtype=F32) / n)


def _cparams(sem, vmem_mb):
    return pltpu.CompilerParams(dimension_semantics=sem, vmem_limit_bytes=vmem_mb * 1024 * 1024)


def _rms(x, g):
    return x * lax.rsqrt(jnp.mean(x * x, axis=-1, keepdims=True) + EPS) * g


def _resident(shape):
    return pl.BlockSpec(shape, lambda *_: (0,) * len(shape), pipeline_mode=pl.Buffered(1))


def _inproj_kernel(x_ref, g_ref, w_ref, b_ref, qkv_ref, gate_ref):
    h = _rms(x_ref[...], g_ref[...]).astype(BF16)
    nq, ng, nc = qkv_ref.shape[1], gate_ref.shape[1], INPROJ_NC
    for c0 in range(0, nq, nc):
        acc = jnp.dot(h, w_ref[:, pl.ds(c0, nc)], preferred_element_type=F32)
        if c0 < DIFF_W:
            acc = acc * (LOG2E * DIFF_HD ** -0.5)
        elif 3 * DIFF_W <= c0 < 3 * DIFF_W + DIL_W:
            acc = acc * (LOG2E * DIL_HD ** -0.5)
        qkv_ref[:, pl.ds(c0, nc)] = acc.astype(qkv_ref.dtype)
    for c0 in range(0, ng, nc):
        acc = jnp.dot(h, w_ref[:, pl.ds(nq + c0, nc)], preferred_element_type=F32)
        gate_ref[:, pl.ds(c0, nc)] = jax.nn.sigmoid(acc + b_ref[:, pl.ds(c0, nc)])


def _inproj(x2d, g, w, b_gate):
    t, d = x2d.shape
    rb = INPROJ_RB
    ng = b_gate.shape[0]
    nq = w.shape[1] - ng
    assert DIFF_W % INPROJ_NC == 0 and DIL_W % INPROJ_NC == 0 and ng % INPROJ_NC == 0 and nq == QKV_W
    return pl.pallas_call(
        _inproj_kernel,
        grid=(t // rb,),
        in_specs=[pl.BlockSpec((rb, d), lambda i: (i, 0)),
                  _resident((1, d)), _resident(w.shape), _resident((1, ng))],
        out_specs=[pl.BlockSpec((rb, nq), lambda i: (i, 0)),
                   pl.BlockSpec((rb, ng), lambda i: (i, 0))],
        out_shape=[jax.ShapeDtypeStruct((t, nq), BF16), jax.ShapeDtypeStruct((t, ng), F32)],
        compiler_params=_cparams(("parallel",), INPROJ_VMEM),
        name="inproj",
    )(x2d, g.reshape(1, d), w, b_gate.reshape(1, ng))


def _diff_kernel(slopes_ref, q_ref, k_ref, v_ref, lq1_ref, lk1_ref, lq2_ref, lk2_ref, subg_ref,
                 o_ref, kaug_ref, vt_ref, acc_ref, ta_ref, tb_ref, *, seq, lam_init):
    qb, kb = DIFF_QB, DIFF_KB
    per = kb // qb
    slope = slopes_ref[pl.program_id(1)] * LOG2E

    lane_k = lax.broadcasted_iota(jnp.int32, (kb, LANES), 1)
    val = slope * lax.broadcasted_iota(jnp.int32, (kb, LANES), 0).astype(F32)
    hi = val.astype(BF16).astype(F32)
    mid = (val - hi).astype(BF16).astype(F32)
    lo = val - hi - mid
    zero_k = jnp.zeros((kb, LANES), F32)

    def bias_cols(first):
        return jnp.where(lane_k == first, hi,
                         jnp.where(lane_k == first + 1, mid, jnp.where(lane_k == first + 2, lo, zero_k)))

    bias1, bias2 = bias_cols(DIFF_HD), bias_cols(0)
    ones_row = jnp.where(lax.broadcasted_iota(jnp.int32, (DIFF_LROWS, kb), 0) == 0, 1.0, 0.0).astype(BF16)

    def build(jb, carry):
        r0 = pl.multiple_of(jb * kb, kb)
        kk = k_ref[pl.ds(r0, kb), :].astype(F32)
        k1 = jnp.where(lane_k < DIFF_HD, kk, bias1)
        k2 = jnp.where(lane_k >= DIFF_HD, kk, bias2)
        kaug_ref[0, jb] = k1.astype(BF16)
        kaug_ref[1, jb] = k2.astype(BF16)
        vt_ref[jb, pl.ds(0, LANES), :] = v_ref[pl.ds(r0, kb), :].astype(F32).T.astype(BF16)
        vt_ref[jb, pl.ds(LANES, DIFF_LROWS), :] = ones_row
        return carry

    lax.fori_loop(0, seq // kb, build, 0)

    lam = (jnp.exp(jnp.sum(lq1_ref[...] * lk1_ref[...], axis=-1, keepdims=True))
           - jnp.exp(jnp.sum(lq2_ref[...] * lk2_ref[...], axis=-1, keepdims=True)) + lam_init)
    lane_q = lax.broadcasted_iota(jnp.int32, (qb, LANES), 1)
    one_q = jnp.ones((qb, LANES), F32)
    zero_q = jnp.zeros((qb, LANES), F32)
    qpos = lax.broadcasted_iota(jnp.int32, (1, qb), 1).astype(F32)
    causal = (lax.broadcasted_iota(jnp.int32, (qb, qb), 0)
              <= lax.broadcasted_iota(jnp.int32, (qb, qb), 1))
    bufs = (ta_ref, tb_ref)

    def scores(q_t, jb, nk, t_ref, diag):
        mx = []
        for c in range(2):
            t = jnp.dot(kaug_ref[c, jb, pl.ds(0, nk), :], q_t[c], preferred_element_type=F32)
            if diag:
                tri = jnp.where(causal, t[nk - qb:, :], NEG)
                t_ref[c, pl.ds(nk - qb, qb), :] = tri
                m = jnp.max(tri, axis=0, keepdims=True)
                if nk > qb:
                    t_ref[c, pl.ds(0, nk - qb), :] = t[:nk - qb, :]
                    m = jnp.maximum(m, jnp.max(t[:nk - qb, :], axis=0, keepdims=True))
            else:
                t_ref[c] = t
                m = jnp.max(t, axis=0, keepdims=True)
            mx.append(m)
        return tuple(mx)

    def accumulate(strip, jb, nk, dist0, t_ref, mx, state):
        u = slope * (jnp.asarray(dist0, F32) + qpos)
        new_state = []
        for c in range(2):
            m_new = mx[c] - u if state is None else jnp.maximum(state[c], mx[c] - u)
            p = jnp.exp2(t_ref[c, pl.ds(0, nk), :] - (m_new + u))
            pv = jnp.dot(vt_ref[jb, :, pl.ds(0, nk)], p.astype(BF16), preferred_element_type=F32)
            if state is None:
                acc_ref[strip, c] = pv
            else:
                acc_ref[strip, c] = jnp.exp2(state[c] - m_new) * acc_ref[strip, c] + pv
            new_state.append(m_new)
        return tuple(new_state)

    def finalize(sb_done):
        for s in range(per):
            l1 = acc_ref[s, 0, pl.ds(LANES, 1), :]
            l2 = acc_ref[s, 1, pl.ds(LANES, 1), :]
            o = (acc_ref[s, 0, pl.ds(0, LANES), :] * (1.0 / l1)
                 - lam * (acc_ref[s, 1, pl.ds(0, LANES), :] * (1.0 / l2)))
            y = o * lax.rsqrt(jnp.mean(o * o, axis=0, keepdims=True) + EPS)
            i0 = pl.multiple_of(sb_done * kb + s * qb, qb)
            o_ref[pl.ds(i0, qb), :] = (y.T * subg_ref[...] * (1.0 - lam_init)).astype(o_ref.dtype)

    def super_block(sb, carry):
        finalize(jnp.maximum(sb - 1, 0))
        q_t = []
        for s in range(per):
            i0 = pl.multiple_of(sb * kb + s * qb, qb)
            qq = q_ref[pl.ds(i0, qb), :].astype(F32)
            q1 = jnp.where(lane_q < DIFF_HD, qq, jnp.where(lane_q < DIFF_HD + 3, one_q, zero_q))
            q2 = jnp.where(lane_q >= DIFF_HD, qq, jnp.where(lane_q < 3, one_q, zero_q))
            q_t.append((q1.T.astype(BF16), q2.T.astype(BF16)))

        state = []
        mx = scores(q_t[0], sb, qb, bufs[0], True)
        for s in range(per):
            if s + 1 < per:
                mx_next = scores(q_t[s + 1], sb, (s + 2) * qb, bufs[(s + 1) % 2], True)
            else:
                mx_next = scores(q_t[0], jnp.maximum(sb - 1, 0), kb, bufs[per % 2], False)
            state.append(accumulate(s, sb, (s + 1) * qb, s * qb, bufs[s % 2], mx, None))
            mx = mx_next

        def full_tile(n, mx, state, more):
            jb = sb - 1 - n
            dist = (n + 1) * kb
            state = list(state)
            for s in range(per):
                if s + 1 < per:
                    mx_next = scores(q_t[s + 1], jb, kb, bufs[(per + s + 1) % 2], False)
                elif more:
                    mx_next = scores(q_t[0], jb - 1, kb, bufs[per % 2], False)
                else:
                    mx_next = mx
                state[s] = accumulate(s, jb, kb, dist + s * qb, bufs[(per + s) % 2], mx, state[s])
                mx = mx_next
            return mx, tuple(state)

        def keep(mx, state):
            return mx, state

        odd = sb % 2
        pairs = sb // 2
        mx, state = lax.cond(
            odd == 1,
            lambda m, st: lax.cond(sb == 1, lambda a, b: full_tile(0, a, b, False),
                                   lambda a, b: full_tile(0, a, b, True), m, st),
            keep, mx, tuple(state))

        def tile_pair(i, carry):
            mx, state = full_tile(odd + 2 * i, *carry, True)
            return full_tile(odd + 2 * i + 1, mx, state, True)

        mx, state = lax.fori_loop(0, jnp.maximum(pairs - 1, 0), tile_pair, (mx, state))

        def last_pair(mx, state):
            n = odd + 2 * (pairs - 1)
            mx, state = full_tile(n, mx, state, True)
            return full_tile(n + 1, mx, state, False)

        lax.cond(pairs >= 1, last_pair, keep, mx, state)
        return carry

    assert per % 2 == 0
    acc_init = jnp.where(lax.broadcasted_iota(jnp.int32, (LANES + DIFF_LROWS, qb), 0) == LANES, 1.0, 0.0)
    for s in range(per):
        for c in range(2):
            acc_ref[s, c] = acc_init
    lax.fori_loop(0, seq // kb, super_block, 0)
    finalize(seq // kb - 1)


def _diff_attn(qkv, lq1, lk1, lq2, lk2, subg, *, lam_init):
    b, s, _ = qkv.shape
    qb, kb = DIFF_QB, DIFF_KB
    small = lambda w: pl.BlockSpec((1, w), lambda bi, h: (0, 0))
    return pl.pallas_call(
        functools.partial(_diff_kernel, seq=s, lam_init=lam_init),
        grid=(b, DIFF_HEADS),
        in_specs=[pl.BlockSpec(memory_space=pltpu.SMEM),
                  pl.BlockSpec((None, s, LANES), lambda bi, h: (bi, 0, h)),
                  pl.BlockSpec((None, s, LANES), lambda bi, h: (bi, 0, COLBLK_DK + h)),
                  pl.BlockSpec((None, s, LANES), lambda bi, h: (bi, 0, COLBLK_DV + h)),
                  small(DIFF_HD), small(DIFF_HD), small(DIFF_HD), small(DIFF_HD), small(2 * DIFF_HD)],
        out_specs=pl.BlockSpec((None, s, LANES), lambda bi, h: (bi, 0, h)),
        out_shape=jax.ShapeDtypeStruct((b, s, DIFF_W), BF16),
        scratch_shapes=[pltpu.VMEM((2, s // kb, kb, LANES), BF16),
                        pltpu.VMEM((s // kb, LANES + DIFF_LROWS, kb), BF16),
                        pltpu.VMEM((kb // qb, 2, LANES + DIFF_LROWS, qb), F32),
                        pltpu.VMEM((2, kb, qb), F32),
                        pltpu.VMEM((2, kb, qb), F32)],
        compiler_params=_cparams(("parallel", "parallel"), DIFF_VMEM),
        name="diff_attn",
    )(_alibi_slopes(DIFF_HEADS), qkv, qkv, qkv,
      lq1.reshape(1, -1), lk1.reshape(1, -1), lq2.reshape(1, -1), lk2.reshape(1, -1), subg.reshape(1, -1))


def _dil_kernel(slopes_ref, *refs):
    ngrp = len(DIL_GROUPS)
    in_refs = refs[:5 * ngrp]
    o_ref, qf_ref, kf_ref, vf_ref, qy_ref, ly_ref, ky_ref, vy_ref, og_ref, lg_ref = refs[5 * ngrp:]
    ch, blk, sub = DIL_CHUNK, DIL_BLK, DIL_SUBSTRIDE
    c = pl.program_id(1)
    slope = slopes_ref[pl.program_id(2)] * LOG2E

    qi = lax.broadcasted_iota(jnp.int32, (blk, 2 * blk), 0)
    ki = lax.broadcasted_iota(jnp.int32, (blk, 2 * blk), 1)
    step = blk + qi - ki
    bias = jnp.where((step >= 0) & (step <= blk), -slope * step.astype(F32), NEG)
    seq_start = jnp.where(ki < blk, jnp.where(c == 0, NEG, 0.0), 0.0)

    def attend(q, kk, vv, first_block):
        s = lax.dot_general(q, kk, (((1,), (1,)), ((), ())), preferred_element_type=F32) + bias
        if first_block:
            s = s + seq_start
        m = jnp.max(s, axis=-1, keepdims=True)
        p = jnp.exp2(s - m)
        den = jnp.sum(p, axis=-1, keepdims=True)
        o = jnp.dot(p.astype(BF16), vv, preferred_element_type=F32) * (1.0 / den)
        return o, jnp.broadcast_to(m + jnp.log2(den), (blk, LANES))

    def run_units(unit):
        for u in range(ch // blk):
            unit(u)

    for g, (window, dil) in enumerate(DIL_GROUPS):
        q_ref, k_ref, v_ref, kp_ref, vp_ref = in_refs[5 * g:5 * g + 5]
        span = blk * dil
        if dil == 1:
            def unit(n, g=g, q_ref=q_ref, k_ref=k_ref, v_ref=v_ref, kp_ref=kp_ref, vp_ref=vp_ref):
                q = q_ref[pl.ds(n * blk, blk), :]
                if n == 0:
                    kk = jnp.concatenate([kp_ref[...], k_ref[pl.ds(0, blk), :]], axis=0)
                    vv = jnp.concatenate([vp_ref[...], v_ref[pl.ds(0, blk), :]], axis=0)
                else:
                    kk = k_ref[pl.ds((n - 1) * blk, 2 * blk), :]
                    vv = v_ref[pl.ds((n - 1) * blk, 2 * blk), :]
                o, lse = attend(q, kk, vv, n == 0)
                og_ref[g, pl.ds(n * blk, blk), :] = o
                lg_ref[g, pl.ds(n * blk, blk), :] = lse

            run_units(unit)
            continue

        qf_ref[...] = q_ref[...].astype(F32)
        kf_ref[pl.ds(ch, ch), :] = k_ref[...].astype(F32)
        vf_ref[pl.ds(ch, ch), :] = v_ref[...].astype(F32)
        kf_ref[pl.ds(ch - span, span), :] = kp_ref[...].astype(F32)
        vf_ref[pl.ds(ch - span, span), :] = vp_ref[...].astype(F32)
        two_stage = dil > sub
        if two_stage:
            assert span == ch and dil % sub == 0
            for r0 in range(sub):
                qy_ref[r0] = qf_ref[pl.ds(r0, ch // sub, stride=sub), :]
                ky_ref[r0] = kf_ref[pl.ds(r0, 2 * ch // sub, stride=sub), :]
                vy_ref[r0] = vf_ref[pl.ds(r0, 2 * ch // sub, stride=sub), :]

        def unit(uidx, g=g, dil=dil, span=span, two_stage=two_stage):
            n_local, r = divmod(uidx, dil)
            q0 = n_local * span + r
            k0 = ch - span + q0
            if two_stage:
                r1, r0 = divmod(r, sub)
                q = qy_ref[r0, pl.ds(r1, blk, stride=dil // sub), :]
                kk = ky_ref[r0, pl.ds(r1, 2 * blk, stride=dil // sub), :]
                vv = vy_ref[r0, pl.ds(r1, 2 * blk, stride=dil // sub), :]
            else:
                q = qf_ref[pl.ds(q0, blk, stride=dil), :]
                kk = kf_ref[pl.ds(k0, 2 * blk, stride=dil), :]
                vv = vf_ref[pl.ds(k0, 2 * blk, stride=dil), :]
            o, lse = attend(q.astype(BF16), kk.astype(BF16), vv.astype(BF16), n_local == 0)
            if two_stage:
                qy_ref[r0, pl.ds(r1, blk, stride=dil // sub), :] = o
                ly_ref[r0, pl.ds(r1, blk, stride=dil // sub), :] = lse
            else:
                og_ref[g, pl.ds(q0, blk, stride=dil), :] = o
                lg_ref[g, pl.ds(q0, blk, stride=dil), :] = lse

        run_units(unit)
        if two_stage:
            for r0 in range(sub):
                og_ref[g, pl.ds(r0, ch // sub, stride=sub), :] = qy_ref[r0]
                lg_ref[g, pl.ds(r0, ch // sub, stride=sub), :] = ly_ref[r0]

    rows = 256
    for r0 in range(0, ch, rows):
        ls = [lg_ref[g, pl.ds(r0, rows), :] for g in range(ngrp)]
        top = functools.reduce(jnp.maximum, ls)
        ws = [jnp.exp2(x - top) for x in ls]
        num = sum(w * og_ref[g, pl.ds(r0, rows), :] for g, w in enumerate(ws))
        o_ref[pl.ds(r0, rows), :] = (num * (1.0 / sum(ws))).astype(o_ref.dtype)


def _dil_attn(qkv):
    b, s, _ = qkv.shape
    ch, blk = DIL_CHUNK, DIL_BLK
    in_specs = [pl.BlockSpec(memory_space=pltpu.SMEM)]
    args = [_alibi_slopes(DIL_HEADS)]
    for g, (window, dil) in enumerate(DIL_GROUPS):
        span = blk * dil
        per = ch // span
        col = g * DIL_HEADS

        def cur(base, col=col):
            return pl.BlockSpec((None, ch, LANES), lambda bi, c, h: (bi, c, base + col + h))

        def prev(base, col=col, span=span, per=per):
            return pl.BlockSpec((None, span, LANES),
                                lambda bi, c, h: (bi, jnp.maximum(c * per - 1, 0), base + col + h))

        in_specs += [cur(COLBLK_SQ), cur(COLBLK_SK), cur(COLBLK_SV), prev(COLBLK_SK), prev(COLBLK_SV)]
        args += [qkv] * 5
    return pl.pallas_call(
        _dil_kernel,
        grid=(b, s // ch, DIL_HEADS),
        in_specs=in_specs,
        out_specs=pl.BlockSpec((None, ch, LANES), lambda bi, c, h: (bi, c, h)),
        out_shape=jax.ShapeDtypeStruct((b, s, DIL_OUT), BF16),
        scratch_shapes=[pltpu.VMEM((ch, LANES), F32),
                        pltpu.VMEM((2 * ch, LANES), F32),
                        pltpu.VMEM((2 * ch, LANES), F32),
                        pltpu.VMEM((DIL_SUBSTRIDE, ch // DIL_SUBSTRIDE, LANES), F32),
                        pltpu.VMEM((DIL_SUBSTRIDE, ch // DIL_SUBSTRIDE, LANES), F32),
                        pltpu.VMEM((DIL_SUBSTRIDE, 2 * ch // DIL_SUBSTRIDE, LANES), F32),
                        pltpu.VMEM((DIL_SUBSTRIDE, 2 * ch // DIL_SUBSTRIDE, LANES), F32),
                        pltpu.VMEM((len(DIL_GROUPS), ch, LANES), F32),
                        pltpu.VMEM((len(DIL_GROUPS), ch, LANES), F32)],
        compiler_params=_cparams(("parallel", "parallel", "parallel"), DIL_VMEM),
        name="dil_attn",
    )(*args)


def _mix_kernel(od_ref, ol_ref, gd_ref, gl_ref, x_ref, wbd_ref, wbl_ref, wo_ref, o_ref):
    yd = jnp.dot(od_ref[...], wbd_ref[...], preferred_element_type=F32)
    yl = jnp.dot(ol_ref[...], wbl_ref[...], preferred_element_type=F32)
    y = gd_ref[...] * yd + gl_ref[...] * yl
    o_ref[...] = x_ref[...] + jnp.dot(y.astype(BF16), wo_ref[...], preferred_element_type=F32)


def _mix_out(od, ol, gates, x2d, wbd, wbl, wo):
    t, d = x2d.shape
    rb = MIX_RB
    return pl.pallas_call(
        _mix_kernel,
        grid=(t // rb,),
        in_specs=[pl.BlockSpec((rb, DIFF_W), lambda i: (i, 0)),
                  pl.BlockSpec((rb, DIL_OUT), lambda i: (i, 0)),
                  pl.BlockSpec((rb, d), lambda i: (i, 0)),
                  pl.BlockSpec((rb, d), lambda i: (i, 1)),
                  pl.BlockSpec((rb, d), lambda i: (i, 0)),
                  _resident(wbd.shape), _resident(wbl.shape), _resident(wo.shape)],
        out_specs=pl.BlockSpec((rb, d), lambda i: (i, 0)),
        out_shape=jax.ShapeDtypeStruct((t, d), F32),
        compiler_params=_cparams(("parallel",), MIX_VMEM),
        name="mix_out",
    )(od, ol, gates, gates, x2d, wbd, wbl, wo)


def _memkv_kernel(mem_ref, g_ref, w_ref, kt_ref, v_ref):
    d = mem_ref.shape[-1]
    mn = _rms(mem_ref[...], g_ref[...]).astype(BF16)
    kv = jnp.dot(mn, w_ref[...], preferred_element_type=F32)
    kt_ref[...] = kv[:, :d].T.astype(BF16)
    v_ref[...] = kv[:, d:].astype(BF16)


def _mem_kv(mem, g, w):
    b, n, d = mem.shape
    return pl.pallas_call(
        _memkv_kernel,
        grid=(b,),
        in_specs=[pl.BlockSpec((None, n, d), lambda bi: (bi, 0, 0)),
                  pl.BlockSpec((1, d), lambda bi: (0, 0)),
                  pl.BlockSpec((d, 2 * d), lambda bi: (0, 0))],
        out_specs=[pl.BlockSpec((None, d, n), lambda bi: (bi, 0, 0)),
                   pl.BlockSpec((None, n, d), lambda bi: (bi, 0, 0))],
        out_shape=[jax.ShapeDtypeStruct((b, d, n), BF16), jax.ShapeDtypeStruct((b, n, d), BF16)],
        compiler_params=_cparams(("parallel",), MEMKV_VMEM),
        name="mem_kv",
    )(mem, g.reshape(1, d), w)


def _cross_kernel(x_ref, g_ref, wq_ref, kt_ref, v_ref, wo_ref, o_ref):
    d = x_ref.shape[-1]
    hd = d // X_HEADS
    x = x_ref[...]
    hq = _rms(x, g_ref[...]).astype(BF16)
    q = (jnp.dot(hq, wq_ref[...], preferred_element_type=F32) * (hd ** -0.5)).astype(BF16)
    outs = []
    for h in range(X_HEADS):
        sl = slice(h * hd, (h + 1) * hd)
        s = jnp.dot(q[:, sl], kt_ref[sl, :], preferred_element_type=F32)
        m = jnp.max(s, axis=-1, keepdims=True)
        p = jnp.exp(s - m)
        den = jnp.sum(p, axis=-1, keepdims=True)
        outs.append(jnp.dot(p.astype(BF16), v_ref[:, sl], preferred_element_type=F32) * (1.0 / den))
    o = jnp.concatenate(outs, axis=-1).astype(BF16)
    o_ref[...] = x + jnp.dot(o, wo_ref[...], preferred_element_type=F32)


def _cross_attn(x1, g, wq, kt, v, wo):
    b, s, d = x1.shape
    rb = CROSS_RB
    n = v.shape[1]
    return pl.pallas_call(
        _cross_kernel,
        grid=(b, s // rb),
        in_specs=[pl.BlockSpec((None, rb, d), lambda bi, i: (bi, i, 0)),
                  _resident((1, d)), _resident((d, d)),
                  pl.BlockSpec((None, d, n), lambda bi, i: (bi, 0, 0)),
                  pl.BlockSpec((None, n, d), lambda bi, i: (bi, 0, 0)),
                  _resident((d, d))],
        out_specs=pl.BlockSpec((None, rb, d), lambda bi, i: (bi, i, 0)),
        out_shape=jax.ShapeDtypeStruct((b, s, d), F32),
        compiler_params=_cparams(("parallel", "parallel"), CROSS_VMEM),
        name="cross_attn",
    )(x1, g.reshape(1, d), wq, kt, v, wo)


def _ffn_kernel(x_ref, g_ref, wu_ref, cw_ref, cb_ref, wd_ref, fg_ref, o_ref, abuf_ref, carry_ref,
                *, blocks_per_seq, final_norm):
    rb = x_ref.shape[0]
    ff = wd_ref.shape[0]
    halo = CONV_HALO
    x = x_ref[...]
    h = _rms(x, g_ref[...]).astype(BF16)
    first = (pl.program_id(0) % blocks_per_seq) == 0
    acc = None
    for c0 in range(0, ff, FFN_CHUNK):
        fc = min(FFN_CHUNK, ff - c0)
        cols = pl.ds(c0, fc)
        a = jnp.dot(h, wu_ref[:, cols], preferred_element_type=F32)
        bgate = jnp.dot(h, wu_ref[:, pl.ds(ff + c0, fc)], preferred_element_type=F32)
        prev = carry_ref[:, cols]
        abuf_ref[pl.ds(0, halo), cols] = jnp.where(first, jnp.zeros_like(prev), prev)
        abuf_ref[pl.ds(halo, rb), cols] = a
        carry_ref[:, cols] = a[rb - halo:, :]
        cw = cw_ref[:, cols]
        y = (cw[2:3, :] * a + cw[1:2, :] * abuf_ref[pl.ds(halo - 1, rb), cols]
             + cw[0:1, :] * abuf_ref[pl.ds(halo - 2, rb), cols] + cb_ref[:, cols])
        act = 0.5 * y * (1.0 + lax.erf(y * (2.0 ** -0.5)))
        part = jnp.dot((act * bgate).astype(BF16), wd_ref[cols, :], preferred_element_type=F32)
        acc = part if acc is None else acc + part
    y = x + acc
    o_ref[...] = _rms(y, fg_ref[...]) if final_norm else y


def _ffn(x2d, g, w_up, cw, cb, w_down, fg, *, seq, final_norm):
    t, d = x2d.shape
    ff = w_down.shape[0]
    rb = FFN_RB
    blocks_per_seq = seq // rb
    assert ff % LANES == 0 and rb >= CONV_HALO and seq % rb == 0
    return pl.pallas_call(
        functools.partial(_ffn_kernel, blocks_per_seq=blocks_per_seq, final_norm=final_norm),
        grid=(t // rb,),
        in_specs=[pl.BlockSpec((rb, d), lambda i: (i, 0)),
                  _resident((1, d)), _resident(w_up.shape), _resident((CONV_W, ff)), _resident((1, ff)),
                  _resident(w_down.shape), _resident((1, d))],
        out_specs=pl.BlockSpec((rb, d), lambda i: (i, 0)),
        out_shape=jax.ShapeDtypeStruct((t, d), F32),
        scratch_shapes=[pltpu.VMEM((rb + CONV_HALO, ff), F32),
                        pltpu.VMEM((CONV_HALO, ff), F32)],
        compiler_params=_cparams(("arbitrary",), FFN_VMEM),
        name="ffn",
    )(x2d, g.reshape(1, d), w_up, cw, cb.reshape(1, ff), w_down, fg.reshape(1, d))


def kernel(x, mem, mix_norm_g, w_in, b_gate, lam_q1, lam_k1, lam_q2, lam_k2, diff_subln_g,
           w_branch_diff, w_branch_dil, w_out, cross_norm_g, mem_norm_g, w_cq, w_ckv, w_co,
           ffn_norm_g, w_up, conv_w, conv_b, w_down, final_norm_g):
    b, s, d = x.shape
    depth = w_in.shape[0]
    t = b * s
    assert s % DIL_CHUNK == 0 and s % DIFF_KB == 0 and DIFF_KB % DIFF_QB == 0 and d == DIFF_W
    assert t % INPROJ_RB == 0 and t % MIX_RB == 0 and s % CROSS_RB == 0
    xf = x.reshape(t, d)
    for l in range(depth):
        lam_init = 0.8 - 0.6 * math.exp(-0.3 * l)
        qkv, gates = _inproj(xf, mix_norm_g[l], w_in[l].astype(BF16), b_gate[l])
        qkv3 = qkv.reshape(b, s, QKV_W)
        o_diff = _diff_attn(qkv3, lam_q1[l], lam_k1[l], lam_q2[l], lam_k2[l], diff_subln_g[l],
                            lam_init=lam_init)
        o_dil = _dil_attn(qkv3)
        x1 = _mix_out(o_diff.reshape(t, DIFF_W), o_dil.reshape(t, DIL_OUT), gates, xf,
                      w_branch_diff[l].astype(BF16), w_branch_dil[l].astype(BF16),
                      w_out[l].astype(BF16))
        mk_t, mv = _mem_kv(mem, mem_norm_g[l], w_ckv[l].astype(BF16))
        x2 = _cross_attn(x1.reshape(b, s, d), cross_norm_g[l], w_cq[l].astype(BF16), mk_t, mv,
                         w_co[l].astype(BF16))
        xf = _ffn(x2.reshape(t, d), ffn_norm_g[l], w_up[l].astype(BF16), conv_w[l], conv_b[l],
                  w_down[l].astype(BF16), final_norm_g, seq=s, final_norm=(l == depth - 1))
    return xf.reshape(b, s, d)
```

```python
import functools
import math

import jax
import jax.numpy as jnp
from jax import lax
from jax.experimental import pallas as pl
from jax.experimental.pallas import tpu as pltpu

F32 = jnp.float32
BF16 = jnp.bfloat16

EPS = 1e-5
NEG = -1e30
LOG2E = math.log2(math.e)
LANES = 128

DIFF_HEADS = 8
DIFF_HD = 64
DIFF_W = DIFF_HEADS * 2 * DIFF_HD
DIFF_QB = 256
DIFF_KB = 1024
DIFF_LROWS = 16
DIL_GROUPS = ((128, 1), (512, 4), (2048, 16))
DIL_HEADS = 4
DIL_HD = 128
DIL_W = len(DIL_GROUPS) * DIL_HEADS * DIL_HD
DIL_OUT = DIL_HEADS * DIL_HD
DIL_BLK = 128
DIL_CHUNK = 2048
DIL_SUBSTRIDE = 4
X_HEADS = 4
CONV_W = 3
CONV_HALO = 8
FFN_CHUNK = 1536

INPROJ_RB, INPROJ_VMEM = 512, 56
DIFF_VMEM = 56
DIL_VMEM = 48
MIX_RB, MIX_VMEM = 1024, 56
MEMKV_VMEM = 32
CROSS_RB, CROSS_VMEM = 1024, 48
FFN_RB, FFN_VMEM = 512, 60

QKV_W = 3 * DIFF_W + 3 * DIL_W
INPROJ_NC = 512
COLBLK_DK = DIFF_W // LANES
COLBLK_DV = 2 * DIFF_W // LANES
COLBLK_SQ = 3 * DIFF_W // LANES
COLBLK_SK = COLBLK_SQ + DIL_W // LANES
COLBLK_SV = COLBLK_SK + DIL_W // LANES


def _alibi_slopes(n):
    return jnp.exp2(-8.0 * jnp.arange(1, n + 1, dtype=F32) / n)


def _cparams(sem, vmem_mb):
    return pltpu.CompilerParams(dimension_semantics=sem, vmem_limit_bytes=vmem_mb * 1024 * 1024)


def _rms(x, g):
    return x * lax.rsqrt(jnp.mean(x * x, axis=-1, keepdims=True) + EPS) * g


def _resident(shape):
    return pl.BlockSpec(shape, lambda *_: (0,) * len(shape), pipeline_mode=pl.Buffered(1))


def _inproj_kernel(x_ref, g_ref, w_ref, qkv_ref):
    h = _rms(x_ref[...], g_ref[...]).astype(BF16)
    nq, nc = qkv_ref.shape[1], INPROJ_NC
    for c0 in range(0, nq, nc):
        acc = jnp.dot(h, w_ref[:, pl.ds(c0, nc)], preferred_element_type=F32)
        if c0 < DIFF_W:
            acc = acc * (LOG2E * DIFF_HD ** -0.5)
        elif 3 * DIFF_W <= c0 < 3 * DIFF_W + DIL_W:
            acc = acc * (LOG2E * DIL_HD ** -0.5)
        qkv_ref[:, pl.ds(c0, nc)] = acc.astype(qkv_ref.dtype)


def _inproj(x2d, g, w):
    t, d = x2d.shape
    rb = INPROJ_RB
    nq = w.shape[1]
    assert DIFF_W % INPROJ_NC == 0 and DIL_W % INPROJ_NC == 0 and nq == QKV_W
    return pl.pallas_call(
        _inproj_kernel,
        grid=(t // rb,),
        in_specs=[pl.BlockSpec((rb, d), lambda i: (i, 0)), _resident((1, d)), _resident(w.shape)],
        out_specs=pl.BlockSpec((rb, nq), lambda i: (i, 0)),
        out_shape=jax.ShapeDtypeStruct((t, nq), BF16),
        compiler_params=_cparams(("parallel",), INPROJ_VMEM),
        name="inproj",
    )(x2d, g.reshape(1, d), w)


def _diff_kernel(slopes_ref, q_ref, k_ref, v_ref, lq1_ref, lk1_ref, lq2_ref, lk2_ref, subg_ref,
                 o_ref, kaug_ref, vt_ref, acc_ref, ta_ref, tb_ref, *, seq, lam_init):
    qb, kb = DIFF_QB, DIFF_KB
    per = kb // qb
    slope = slopes_ref[pl.program_id(1)] * LOG2E

    lane_k = lax.broadcasted_iota(jnp.int32, (kb, LANES), 1)
    val = slope * lax.broadcasted_iota(jnp.int32, (kb, LANES), 0).astype(F32)
    hi = val.astype(BF16).astype(F32)
    mid = (val - hi).astype(BF16).astype(F32)
    lo = val - hi - mid
    zero_k = jnp.zeros((kb, LANES), F32)

    def bias_cols(first):
        return jnp.where(lane_k == first, hi,
                         jnp.where(lane_k == first + 1, mid, jnp.where(lane_k == first + 2, lo, zero_k)))

    bias1, bias2 = bias_cols(DIFF_HD), bias_cols(0)
    ones_row = jnp.where(lax.broadcasted_iota(jnp.int32, (DIFF_LROWS, kb), 0) == 0, 1.0, 0.0).astype(BF16)

    def build(jb, carry):
        r0 = pl.multiple_of(jb * kb, kb)
        kk = k_ref[pl.ds(r0, kb), :].astype(F32)
        k1 = jnp.where(lane_k < DIFF_HD, kk, bias1)
        k2 = jnp.where(lane_k >= DIFF_HD, kk, bias2)
        kaug_ref[0, jb] = k1.astype(BF16)
        kaug_ref[1, jb] = k2.astype(BF16)
        vt_ref[jb, pl.ds(0, LANES), :] = v_ref[pl.ds(r0, kb), :].astype(F32).T.astype(BF16)
        vt_ref[jb, pl.ds(LANES, DIFF_LROWS), :] = ones_row
        return carry

    lax.fori_loop(0, seq // kb, build, 0)

    lam = (jnp.exp(jnp.sum(lq1_ref[...] * lk1_ref[...], axis=-1, keepdims=True))
           - jnp.exp(jnp.sum(lq2_ref[...] * lk2_ref[...], axis=-1, keepdims=True)) + lam_init)
    lane_q = lax.broadcasted_iota(jnp.int32, (qb, LANES), 1)
    one_q = jnp.ones((qb, LANES), F32)
    zero_q = jnp.zeros((qb, LANES), F32)
    qpos = lax.broadcasted_iota(jnp.int32, (1, qb), 1).astype(F32)
    causal = (lax.broadcasted_iota(jnp.int32, (qb, qb), 0)
              <= lax.broadcasted_iota(jnp.int32, (qb, qb), 1))
    bufs = (ta_ref, tb_ref)

    def scores(q_t, jb, nk, t_ref, diag):
        mx = []
        for c in range(2):
            t = jnp.dot(kaug_ref[c, jb, pl.ds(0, nk), :], q_t[c], preferred_element_type=F32)
            if diag:
                tri = jnp.where(causal, t[nk - qb:, :], NEG)
                t_ref[c, pl.ds(nk - qb, qb), :] = tri
                m = jnp.max(tri, axis=0, keepdims=True)
                if nk > qb:
                    t_ref[c, pl.ds(0, nk - qb), :] = t[:nk - qb, :]
                    m = jnp.maximum(m, jnp.max(t[:nk - qb, :], axis=0, keepdims=True))
            else:
                t_ref[c] = t
                m = jnp.max(t, axis=0, keepdims=True)
            mx.append(m)
        return tuple(mx)

    def accumulate(strip, jb, nk, dist0, t_ref, mx, state):
        u = slope * (jnp.asarray(dist0, F32) + qpos)
        new_state = []
        for c in range(2):
            m_new = mx[c] - u if state is None else jnp.maximum(state[c], mx[c] - u)
            p = jnp.exp2(t_ref[c, pl.ds(0, nk), :] - (m_new + u))
            pv = jnp.dot(vt_ref[jb, :, pl.ds(0, nk)], p.astype(BF16), preferred_element_type=F32)
            if state is None:
                acc_ref[strip, c] = pv
            else:
                acc_ref[strip, c] = jnp.exp2(state[c] - m_new) * acc_ref[strip, c] + pv
            new_state.append(m_new)
        return tuple(new_state)

    def finalize(sb_done):
        for s in range(per):
            l1 = acc_ref[s, 0, pl.ds(LANES, 1), :]
            l2 = acc_ref[s, 1, pl.ds(LANES, 1), :]
            o = (acc_ref[s, 0, pl.ds(0, LANES), :] * (1.0 / l1)
                 - lam * (acc_ref[s, 1, pl.ds(0, LANES), :] * (1.0 / l2)))
            y = o * lax.rsqrt(jnp.mean(o * o, axis=0, keepdims=True) + EPS)
            i0 = pl.multiple_of(sb_done * kb + s * qb, qb)
            o_ref[pl.ds(i0, qb), :] = (y.T * subg_ref[...] * (1.0 - lam_init)).astype(o_ref.dtype)

    def super_block(sb, carry):
        finalize(jnp.maximum(sb - 1, 0))
        q_t = []
        for s in range(per):
            i0 = pl.multiple_of(sb * kb + s * qb, qb)
            qq = q_ref[pl.ds(i0, qb), :].astype(F32)
            q1 = jnp.where(lane_q < DIFF_HD, qq, jnp.where(lane_q < DIFF_HD + 3, one_q, zero_q))
            q2 = jnp.where(lane_q >= DIFF_HD, qq, jnp.where(lane_q < 3, one_q, zero_q))
            q_t.append((q1.T.astype(BF16), q2.T.astype(BF16)))

        state = []
        mx = scores(q_t[0], sb, qb, bufs[0], True)
        for s in range(per):
            if s + 1 < per:
                mx_next = scores(q_t[s + 1], sb, (s + 2) * qb, bufs[(s + 1) % 2], True)
            else:
                mx_next = scores(q_t[0], jnp.maximum(sb - 1, 0), kb, bufs[per % 2], False)
            state.append(accumulate(s, sb, (s + 1) * qb, s * qb, bufs[s % 2], mx, None))
            mx = mx_next

        def full_tile(n, mx, state, more):
            jb = sb - 1 - n
            dist = (n + 1) * kb
            state = list(state)
            for s in range(per):
                if s + 1 < per:
                    mx_next = scores(q_t[s + 1], jb, kb, bufs[(per + s + 1) % 2], False)
                elif more:
                    mx_next = scores(q_t[0], jb - 1, kb, bufs[per % 2], False)
                else:
                    mx_next = mx
                state[s] = accumulate(s, jb, kb, dist + s * qb, bufs[(per + s) % 2], mx, state[s])
                mx = mx_next
            return mx, tuple(state)

        def keep(mx, state):
            return mx, state

        odd = sb % 2
        pairs = sb // 2
        mx, state = lax.cond(
            odd == 1,
            lambda m, st: lax.cond(sb == 1, lambda a, b: full_tile(0, a, b, False),
                                   lambda a, b: full_tile(0, a, b, True), m, st),
            keep, mx, tuple(state))

        def tile_pair(i, carry):
            mx, state = full_tile(odd + 2 * i, *carry, True)
            return full_tile(odd + 2 * i + 1, mx, state, True)

        mx, state = lax.fori_loop(0, jnp.maximum(pairs - 1, 0), tile_pair, (mx, state))

        def last_pair(mx, state):
            n = odd + 2 * (pairs - 1)
            mx, state = full_tile(n, mx, state, True)
            return full_tile(n + 1, mx, state, False)

        lax.cond(pairs >= 1, last_pair, keep, mx, state)
        return carry

    assert per % 2 == 0
    acc_init = jnp.where(lax.broadcasted_iota(jnp.int32, (LANES + DIFF_LROWS, qb), 0) == LANES, 1.0, 0.0)
    for s in range(per):
        for c in range(2):
            acc_ref[s, c] = acc_init
    lax.fori_loop(0, seq // kb, super_block, 0)
    finalize(seq // kb - 1)


def _diff_attn(qkv, lq1, lk1, lq2, lk2, subg, *, lam_init):
    b, s, _ = qkv.shape
    qb, kb = DIFF_QB, DIFF_KB
    small = lambda w: pl.BlockSpec((1, w), lambda bi, h: (0, 0))
    return pl.pallas_call(
        functools.partial(_diff_kernel, seq=s, lam_init=lam_init),
        grid=(b, DIFF_HEADS),
        in_specs=[pl.BlockSpec(memory_space=pltpu.SMEM),
                  pl.BlockSpec((None, s, LANES), lambda bi, h: (bi, 0, h)),
                  pl.BlockSpec((None, s, LANES), lambda bi, h: (bi, 0, COLBLK_DK + h)),
                  pl.BlockSpec((None, s, LANES), lambda bi, h: (bi, 0, COLBLK_DV + h)),
                  small(DIFF_HD), small(DIFF_HD), small(DIFF_HD), small(DIFF_HD), small(2 * DIFF_HD)],
        out_specs=pl.BlockSpec((None, s, LANES), lambda bi, h: (bi, 0, h)),
        out_shape=jax.ShapeDtypeStruct((b, s, DIFF_W), BF16),
        scratch_shapes=[pltpu.VMEM((2, s // kb, kb, LANES), BF16),
                        pltpu.VMEM((s // kb, LANES + DIFF_LROWS, kb), BF16),
                        pltpu.VMEM((kb // qb, 2, LANES + DIFF_LROWS, qb), F32),
                        pltpu.VMEM((2, kb, qb), F32),
                        pltpu.VMEM((2, kb, qb), F32)],
        compiler_params=_cparams(("parallel", "parallel"), DIFF_VMEM),
        name="diff_attn",
    )(_alibi_slopes(DIFF_HEADS), qkv, qkv, qkv,
      lq1.reshape(1, -1), lk1.reshape(1, -1), lq2.reshape(1, -1), lk2.reshape(1, -1), subg.reshape(1, -1))


def _dil_kernel(slopes_ref, *refs):
    ngrp = len(DIL_GROUPS)
    in_refs = refs[:5 * ngrp]
    o_ref, qf_ref, kf_ref, vf_ref, qy_ref, ly_ref, ky_ref, vy_ref, og_ref, lg_ref = refs[5 * ngrp:]
    ch, blk, sub = DIL_CHUNK, DIL_BLK, DIL_SUBSTRIDE
    c = pl.program_id(1)
    slope = slopes_ref[pl.program_id(2)] * LOG2E

    qi = lax.broadcasted_iota(jnp.int32, (blk, 2 * blk), 0)
    ki = lax.broadcasted_iota(jnp.int32, (blk, 2 * blk), 1)
    step = blk + qi - ki
    bias = jnp.where((step >= 0) & (step <= blk), -slope * step.astype(F32), NEG)
    seq_start = jnp.where(ki < blk, jnp.where(c == 0, NEG, 0.0), 0.0)

    def attend(q, kk, vv, first_block):
        s = lax.dot_general(q, kk, (((1,), (1,)), ((), ())), preferred_element_type=F32) + bias
        if first_block:
            s = s + seq_start
        m = jnp.max(s, axis=-1, keepdims=True)
        p = jnp.exp2(s - m)
        den = jnp.sum(p, axis=-1, keepdims=True)
        o = jnp.dot(p.astype(BF16), vv, preferred_element_type=F32) * (1.0 / den)
        return o, jnp.broadcast_to(m + jnp.log2(den), (blk, LANES))

    def run_units(unit):
        for u in range(ch // blk):
            unit(u)

    for g, (window, dil) in enumerate(DIL_GROUPS):
        q_ref, k_ref, v_ref, kp_ref, vp_ref = in_refs[5 * g:5 * g + 5]
        span = blk * dil
        if dil == 1:
            def unit(n, g=g, q_ref=q_ref, k_ref=k_ref, v_ref=v_ref, kp_ref=kp_ref, vp_ref=vp_ref):
                q = q_ref[pl.ds(n * blk, blk), :]
                if n == 0:
                    kk = jnp.concatenate([kp_ref[...], k_ref[pl.ds(0, blk), :]], axis=0)
                    vv = jnp.concatenate([vp_ref[...], v_ref[pl.ds(0, blk), :]], axis=0)
                else:
                    kk = k_ref[pl.ds((n - 1) * blk, 2 * blk), :]
                    vv = v_ref[pl.ds((n - 1) * blk, 2 * blk), :]
                o, lse = attend(q, kk, vv, n == 0)
                og_ref[g, pl.ds(n * blk, blk), :] = o
                lg_ref[g, pl.ds(n * blk, blk), :] = lse

            run_units(unit)
            continue

        qf_ref[...] = q_ref[...].astype(F32)
        kf_ref[pl.ds(ch, ch), :] = k_ref[...].astype(F32)
        vf_ref[pl.ds(ch, ch), :] = v_ref[...].astype(F32)
        kf_ref[pl.ds(ch - span, span), :] = kp_ref[...].astype(F32)
        vf_ref[pl.ds(ch - span, span), :] = vp_ref[...].astype(F32)
        two_stage = dil > sub
        if two_stage:
            assert span == ch and dil % sub == 0
            for r0 in range(sub):
                qy_ref[r0] = qf_ref[pl.ds(r0, ch // sub, stride=sub), :]
                ky_ref[r0] = kf_ref[pl.ds(r0, 2 * ch // sub, stride=sub), :]
                vy_ref[r0] = vf_ref[pl.ds(r0, 2 * ch // sub, stride=sub), :]

        def unit(uidx, g=g, dil=dil, span=span, two_stage=two_stage):
            n_local, r = divmod(uidx, dil)
            q0 = n_local * span + r
            k0 = ch - span + q0
            if two_stage:
                r1, r0 = divmod(r, sub)
                q = qy_ref[r0, pl.ds(r1, blk, stride=dil // sub), :]
                kk = ky_ref[r0, pl.ds(r1, 2 * blk, stride=dil // sub), :]
                vv = vy_ref[r0, pl.ds(r1, 2 * blk, stride=dil // sub), :]
            else:
                q = qf_ref[pl.ds(q0, blk, stride=dil), :]
                kk = kf_ref[pl.ds(k0, 2 * blk, stride=dil), :]
                vv = vf_ref[pl.ds(k0, 2 * blk, stride=dil), :]
            o, lse = attend(q.astype(BF16), kk.astype(BF16), vv.astype(BF16), n_local == 0)
            if two_stage:
                qy_ref[r0, pl.ds(r1, blk, stride=dil // sub), :] = o
                ly_ref[r0, pl.ds(r1, blk, stride=dil // sub), :] = lse
            else:
                og_ref[g, pl.ds(q0, blk, stride=dil), :] = o
                lg_ref[g, pl.ds(q0, blk, stride=dil), :] = lse

        run_units(unit)
        if two_stage:
            for r0 in range(sub):
                og_ref[g, pl.ds(r0, ch // sub, stride=sub), :] = qy_ref[r0]
                lg_ref[g, pl.ds(r0, ch // sub, stride=sub), :] = ly_ref[r0]

    rows = 256
    for r0 in range(0, ch, rows):
        ls = [lg_ref[g, pl.ds(r0, rows), :] for g in range(ngrp)]
        top = functools.reduce(jnp.maximum, ls)
        ws = [jnp.exp2(x - top) for x in ls]
        num = sum(w * og_ref[g, pl.ds(r0, rows), :] for g, w in enumerate(ws))
        o_ref[pl.ds(r0, rows), :] = (num * (1.0 / sum(ws))).astype(o_ref.dtype)


def _dil_attn(qkv):
    b, s, _ = qkv.shape
    ch, blk = DIL_CHUNK, DIL_BLK
    in_specs = [pl.BlockSpec(memory_space=pltpu.SMEM)]
    args = [_alibi_slopes(DIL_HEADS)]
    for g, (window, dil) in enumerate(DIL_GROUPS):
        span = blk * dil
        per = ch // span
        col = g * DIL_HEADS

        def cur(base, col=col):
            return pl.BlockSpec((None, ch, LANES), lambda bi, c, h: (bi, c, base + col + h))

        def prev(base, col=col, span=span, per=per):
            return pl.BlockSpec((None, span, LANES),
                                lambda bi, c, h: (bi, jnp.maximum(c * per - 1, 0), base + col + h))

        in_specs += [cur(COLBLK_SQ), cur(COLBLK_SK), cur(COLBLK_SV), prev(COLBLK_SK), prev(COLBLK_SV)]
        args += [qkv] * 5
    return pl.pallas_call(
        _dil_kernel,
        grid=(b, s // ch, DIL_HEADS),
        in_specs=in_specs,
        out_specs=pl.BlockSpec((None, ch, LANES), lambda bi, c, h: (bi, c, h)),
        out_shape=jax.ShapeDtypeStruct((b, s, DIL_OUT), BF16),
        scratch_shapes=[pltpu.VMEM((ch, LANES), F32),
                        pltpu.VMEM((2 * ch, LANES), F32),
                        pltpu.VMEM((2 * ch, LANES), F32),
                        pltpu.VMEM((DIL_SUBSTRIDE, ch // DIL_SUBSTRIDE, LANES), F32),
                        pltpu.VMEM((DIL_SUBSTRIDE, ch // DIL_SUBSTRIDE, LANES), F32),
                        pltpu.VMEM((DIL_SUBSTRIDE, 2 * ch // DIL_SUBSTRIDE, LANES), F32),
                        pltpu.VMEM((DIL_SUBSTRIDE, 2 * ch // DIL_SUBSTRIDE, LANES), F32),
                        pltpu.VMEM((len(DIL_GROUPS), ch, LANES), F32),
                        pltpu.VMEM((len(DIL_GROUPS), ch, LANES), F32)],
        compiler_params=_cparams(("parallel", "parallel", "parallel"), DIL_VMEM),
        name="dil_attn",
    )(*args)


def _mix_kernel(od_ref, ol_ref, x_ref, g_ref, wg_ref, bg_ref, wbd_ref, wbl_ref, wo_ref, o_ref):
    d = x_ref.shape[1]
    x = x_ref[...]
    h = _rms(x, g_ref[...]).astype(BF16)
    gd = jax.nn.sigmoid(jnp.dot(h, wg_ref[:, pl.ds(0, d)], preferred_element_type=F32) + bg_ref[:, pl.ds(0, d)])
    yd = gd * jnp.dot(od_ref[...], wbd_ref[...], preferred_element_type=F32)
    gl = jax.nn.sigmoid(jnp.dot(h, wg_ref[:, pl.ds(d, d)], preferred_element_type=F32) + bg_ref[:, pl.ds(d, d)])
    y = yd + gl * jnp.dot(ol_ref[...], wbl_ref[...], preferred_element_type=F32)
    o_ref[...] = x + jnp.dot(y.astype(BF16), wo_ref[...], preferred_element_type=F32)


def _mix_out(od, ol, x2d, g, w_gate, b_gate, wbd, wbl, wo):
    t, d = x2d.shape
    rb = MIX_RB
    return pl.pallas_call(
        _mix_kernel,
        grid=(t // rb,),
        in_specs=[pl.BlockSpec((rb, DIFF_W), lambda i: (i, 0)),
                  pl.BlockSpec((rb, DIL_OUT), lambda i: (i, 0)),
                  pl.BlockSpec((rb, d), lambda i: (i, 0)),
                  _resident((1, d)), _resident(w_gate.shape), _resident((1, 2 * d)),
                  _resident(wbd.shape), _resident(wbl.shape), _resident(wo.shape)],
        out_specs=pl.BlockSpec((rb, d), lambda i: (i, 0)),
        out_shape=jax.ShapeDtypeStruct((t, d), F32),
        compiler_params=_cparams(("parallel",), MIX_VMEM),
        name="mix_out",
    )(od, ol, x2d, g.reshape(1, d), w_gate, b_gate.reshape(1, 2 * d), wbd, wbl, wo)


def _memkv_kernel(mem_ref, g_ref, w_ref, kt_ref, v_ref):
    d = mem_ref.shape[-1]
    mn = _rms(mem_ref[...], g_ref[...]).astype(BF16)
    kv = jnp.dot(mn, w_ref[...], preferred_element_type=F32)
    kt_ref[...] = kv[:, :d].T.astype(BF16)
    v_ref[...] = kv[:, d:].astype(BF16)


def _mem_kv(mem, g, w):
    b, n, d = mem.shape
    return pl.pallas_call(
        _memkv_kernel,
        grid=(b,),
        in_specs=[pl.BlockSpec((None, n, d), lambda bi: (bi, 0, 0)),
                  pl.BlockSpec((1, d), lambda bi: (0, 0)),
                  pl.BlockSpec((d, 2 * d), lambda bi: (0, 0))],
        out_specs=[pl.BlockSpec((None, d, n), lambda bi: (bi, 0, 0)),
                   pl.BlockSpec((None, n, d), lambda bi: (bi, 0, 0))],
        out_shape=[jax.ShapeDtypeStruct((b, d, n), BF16), jax.ShapeDtypeStruct((b, n, d), BF16)],
        compiler_params=_cparams(("parallel",), MEMKV_VMEM),
        name="mem_kv",
    )(mem, g.reshape(1, d), w)


def _cross_kernel(x_ref, g_ref, wq_ref, kt_ref, v_ref, wo_ref, o_ref):
    d = x_ref.shape[-1]
    hd = d // X_HEADS
    x = x_ref[...]
    hq = _rms(x, g_ref[...]).astype(BF16)
    q = (jnp.dot(hq, wq_ref[...], preferred_element_type=F32) * (hd ** -0.5)).astype(BF16)
    outs = []
    for h in range(X_HEADS):
        sl = slice(h * hd, (h + 1) * hd)
        s = jnp.dot(q[:, sl], kt_ref[sl, :], preferred_element_type=F32)
        m = jnp.max(s, axis=-1, keepdims=True)
        p = jnp.exp(s - m)
        den = jnp.sum(p, axis=-1, keepdims=True)
        outs.append(jnp.dot(p.astype(BF16), v_ref[:, sl], preferred_element_type=F32) * (1.0 / den))
    o = jnp.concatenate(outs, axis=-1).astype(BF16)
    o_ref[...] = x + jnp.dot(o, wo_ref[...], preferred_element_type=F32)


def _cross_attn(x1, g, wq, kt, v, wo):
    b, s, d = x1.shape
    rb = CROSS_RB
    n = v.shape[1]
    return pl.pallas_call(
        _cross_kernel,
        grid=(b, s // rb),
        in_specs=[pl.BlockSpec((None, rb, d), lambda bi, i: (bi, i, 0)),
                  _resident((1, d)), _resident((d, d)),
                  pl.BlockSpec((None, d, n), lambda bi, i: (bi, 0, 0)),
                  pl.BlockSpec((None, n, d), lambda bi, i: (bi, 0, 0)),
                  _resident((d, d))],
        out_specs=pl.BlockSpec((None, rb, d), lambda bi, i: (bi, i, 0)),
        out_shape=jax.ShapeDtypeStruct((b, s, d), F32),
        compiler_params=_cparams(("parallel", "parallel"), CROSS_VMEM),
        name="cross_attn",
    )(x1, g.reshape(1, d), wq, kt, v, wo)


def _ffn_kernel(x_ref, g_ref, wu_ref, cw_ref, cb_ref, wd_ref, fg_ref, o_ref, abuf_ref, carry_ref,
                *, blocks_per_seq, final_norm):
    rb = x_ref.shape[0]
    ff = wd_ref.shape[0]
    halo = CONV_HALO
    x = x_ref[...]
    h = _rms(x, g_ref[...]).astype(BF16)
    first = (pl.program_id(0) % blocks_per_seq) == 0
    acc = None
    for c0 in range(0, ff, FFN_CHUNK):
        fc = min(FFN_CHUNK, ff - c0)
        cols = pl.ds(c0, fc)
        a = jnp.dot(h, wu_ref[:, cols], preferred_element_type=F32)
        bgate = jnp.dot(h, wu_ref[:, pl.ds(ff + c0, fc)], preferred_element_type=F32)
        prev = carry_ref[:, cols]
        abuf_ref[pl.ds(0, halo), cols] = jnp.where(first, jnp.zeros_like(prev), prev)
        abuf_ref[pl.ds(halo, rb), cols] = a
        carry_ref[:, cols] = a[rb - halo:, :]
        cw = cw_ref[:, cols]
        y = (cw[2:3, :] * a + cw[1:2, :] * abuf_ref[pl.ds(halo - 1, rb), cols]
             + cw[0:1, :] * abuf_ref[pl.ds(halo - 2, rb), cols] + cb_ref[:, cols])
        act = 0.5 * y * (1.0 + lax.erf(y * (2.0 ** -0.5)))
        part = jnp.dot((act * bgate).astype(BF16), wd_ref[cols, :], preferred_element_type=F32)
        acc = part if acc is None else acc + part
    y = x + acc
    o_ref[...] = _rms(y, fg_ref[...]) if final_norm else y


def _ffn(x2d, g, w_up, cw, cb, w_down, fg, *, seq, final_norm):
    t, d = x2d.shape
    ff = w_down.shape[0]
    rb = FFN_RB
    blocks_per_seq = seq // rb
    assert ff % LANES == 0 and rb >= CONV_HALO and seq % rb == 0
    return pl.pallas_call(
        functools.partial(_ffn_kernel, blocks_per_seq=blocks_per_seq, final_norm=final_norm),
        grid=(t // rb,),
        in_specs=[pl.BlockSpec((rb, d), lambda i: (i, 0)),
                  _resident((1, d)), _resident(w_up.shape), _resident((CONV_W, ff)), _resident((1, ff)),
                  _resident(w_down.shape), _resident((1, d))],
        out_specs=pl.BlockSpec((rb, d), lambda i: (i, 0)),
        out_shape=jax.ShapeDtypeStruct((t, d), F32),
        scratch_shapes=[pltpu.VMEM((rb + CONV_HALO, ff), F32),
                        pltpu.VMEM((CONV_HALO, ff), F32)],
        compiler_params=_cparams(("arbitrary",), FFN_VMEM),
        name="ffn",
    )(x2d, g.reshape(1, d), w_up, cw, cb.reshape(1, ff), w_down, fg.reshape(1, d))


def kernel(x, mem, mix_norm_g, w_in, b_gate, lam_q1, lam_k1, lam_q2, lam_k2, diff_subln_g,
           w_branch_diff, w_branch_dil, w_out, cross_norm_g, mem_norm_g, w_cq, w_ckv, w_co,
           ffn_norm_g, w_up, conv_w, conv_b, w_down, final_norm_g):
    b, s, d = x.shape
    depth = w_in.shape[0]
    t = b * s
    assert s % DIL_CHUNK == 0 and s % DIFF_KB == 0 and DIFF_KB % DIFF_QB == 0 and d == DIFF_W
    assert t % INPROJ_RB == 0 and t % MIX_RB == 0 and s % CROSS_RB == 0
    xf = x.reshape(t, d)
    for l in range(depth):
        lam_init = 0.8 - 0.6 * math.exp(-0.3 * l)
        w_in_l = w_in[l].astype(BF16)
        qkv = _inproj(xf, mix_norm_g[l], w_in_l[:, :QKV_W])
        qkv3 = qkv.reshape(b, s, QKV_W)
        o_diff = _diff_attn(qkv3, lam_q1[l], lam_k1[l], lam_q2[l], lam_k2[l], diff_subln_g[l],
                            lam_init=lam_init)
        o_dil = _dil_attn(qkv3)
        x1 = _mix_out(o_diff.reshape(t, DIFF_W), o_dil.reshape(t, DIL_OUT), xf, mix_norm_g[l],
                      w_in_l[:, QKV_W:], b_gate[l],
                      w_branch_diff[l].astype(BF16), w_branch_dil[l].astype(BF16),
                      w_out[l].astype(BF16))
        mk_t, mv = _mem_kv(mem, mem_norm_g[l], w_ckv[l].astype(BF16))
        x2 = _cross_attn(x1.reshape(b, s, d), cross_norm_g[l], w_cq[l].astype(BF16), mk_t, mv,
                         w_co[l].astype(BF16))
        xf = _ffn(x2.reshape(t, d), ffn_norm_g[l], w_up[l].astype(BF16), conv_w[l], conv_b[l],
                  w_down[l].astype(BF16), final_norm_g, seq=s, final_norm=(l == depth - 1))
    return xf.reshape(b, s, d)
```
